```python
import math
import jax, jax.numpy as jnp
from jax import lax
import numpy as np

D_MODEL = 1024
BATCH = 16
SEQ = 2048
DEPTH = 1

MIX_WIDTH = D_MODEL
ATTN_WIDTH = D_MODEL // 2
SGU_WIDTH = MIX_WIDTH - ATTN_WIDTH
HEAD_DIM = 64
N_HEADS = ATTN_WIDTH // HEAD_DIM
N_KV = 2
GQ = N_HEADS // N_KV
KV_WIDTH = N_KV * HEAD_DIM
WINDOW = 128
BLK = 128
SGU_GROUPS = 8
SGU_GROUP_DIM = SGU_WIDTH // SGU_GROUPS
CHUNK = 128
D_FF = ((8 * D_MODEL // 3) + 127) // 128 * 128
CONV_WIDTH = 3
PLE_DIM = 256
IN_WIDTH = ATTN_WIDTH + 2 * KV_WIDTH + 2 * SGU_WIDTH
RMS_EPS = 1e-6
LN_EPS = 1e-5
NEG_INF = -1e30

kernel_name = "hymba_style_swa_sgu_convffn_ple_encoder"


def rmsnorm(x, g):
    xf = x.astype(jnp.float32)
    y = xf * lax.rsqrt(jnp.mean(xf * xf, axis=-1, keepdims=True) + RMS_EPS)
    return (y * g.astype(jnp.float32)).astype(x.dtype)


def layernorm(x, g, b):
    xf = x.astype(jnp.float32)
    mu = jnp.mean(xf, axis=-1, keepdims=True)
    var = jnp.mean(jnp.square(xf - mu), axis=-1, keepdims=True)
    y = (xf - mu) * lax.rsqrt(var + LN_EPS)
    return (y * g.astype(jnp.float32) + b.astype(jnp.float32)).astype(x.dtype)


def alibi_slopes(n_heads):
    return jnp.exp2(-8.0 * jnp.arange(1, n_heads + 1, dtype=jnp.float32) / n_heads)


def band_blocks(t, nb):
    b = t.shape[0]
    tp = jnp.pad(t, ((0, 0), (BLK, BLK), (0, 0), (0, 0)))
    tp = tp.reshape(b, nb + 2, BLK, t.shape[2], t.shape[3])
    return jnp.concatenate([tp[:, :-2], tp[:, 1:-1], tp[:, 2:]], axis=2)


def windowed_gqa(q, k, v, sink):
    b, s = q.shape[0], q.shape[1]
    nb = s // BLK
    qb = q.reshape(b, nb, BLK, N_KV, GQ, HEAD_DIM)
    kb = band_blocks(k, nb)
    vb = band_blocks(v, nb)
    scores = jnp.einsum('bnqkgd,bnskd->bnkgqs', qb, kb).astype(jnp.float32)
    scores = scores * (HEAD_DIM ** -0.5)
    qi = jnp.arange(BLK)[:, None]
    kj = jnp.arange(3 * BLK)[None, :]
    dist = jnp.abs(qi + BLK - kj)
    key_pos = jnp.arange(nb)[:, None] * BLK - BLK + jnp.arange(3 * BLK)[None, :]
    valid = (dist <= WINDOW)[None] & ((key_pos >= 0) & (key_pos < s))[:, None, :]
    slopes = alibi_slopes(N_HEADS).reshape(N_KV, GQ)
    bias = -slopes[:, :, None, None] * dist.astype(jnp.float32)
    scores = jnp.where(valid[None, :, None, None], scores + bias[None, None], NEG_INF)
    sink_l = sink.astype(jnp.float32).reshape(N_KV, GQ)[None, None, :, :, None, None]
    m = jnp.maximum(jnp.max(scores, axis=-1, keepdims=True), sink_l)
    e = jnp.exp(scores - m)
    denom = jnp.sum(e, axis=-1, keepdims=True) + jnp.exp(sink_l - m)
    probs = (e / denom).astype(v.dtype)
    o = jnp.einsum('bnkgqs,bnskd->bnqkgd', probs, vb)
    return o.reshape(b, s, N_HEADS * HEAD_DIM)


def chunked_sgu(zu, zv, ln_g, ln_b, w_s, b_s):
    b, s = zu.shape[0], zu.shape[1]
    nc = s // CHUNK
    u = jax.nn.gelu(zu, approximate=True)
    vv = layernorm(jax.nn.gelu(zv, approximate=True), ln_g, ln_b)
    vv = vv.reshape(b, nc, CHUNK, SGU_GROUPS, SGU_GROUP_DIM)
    mixed = jnp.einsum('hts,bnshc->bnthc', w_s, vv) + b_s.T[None, None, :, :, None]
    return u * mixed.reshape(b, s, SGU_WIDTH)


def dwconv3_centred(h, w, bias):
    hp = jnp.pad(h, ((0, 0), (1, 1), (0, 0)))
    return hp[:, :-2] * w[0] + hp[:, 1:-1] * w[1] + hp[:, 2:] * w[2] + bias


def setup_inputs(seed: int = 0) -> dict:
    key = jax.random.key(seed)
    ks = jax.random.split(key, 24)
    f32 = jnp.float32
    nrm = lambda k, shape, scale: jax.random.normal(k, shape, f32) * scale
    gain = lambda k, shape: 1.0 + 0.02 * jax.random.normal(k, shape, f32)
    return {
        "x": jax.random.normal(ks[0], (BATCH, SEQ, D_MODEL), f32),
        "p": jax.random.normal(ks[1], (DEPTH, BATCH, SEQ, PLE_DIM), f32),
        "g_mix": gain(ks[2], (DEPTH, D_MODEL)),
        "w_in": nrm(ks[3], (DEPTH, D_MODEL, IN_WIDTH), D_MODEL ** -0.5),
        "attn_sink": nrm(ks[4], (DEPTH, N_HEADS), 0.5),
        "sgu_ln_g": gain(ks[5], (DEPTH, SGU_WIDTH)),
        "sgu_ln_b": nrm(ks[6], (DEPTH, SGU_WIDTH), 0.02),
        "sgu_w": nrm(ks[7], (DEPTH, SGU_GROUPS, CHUNK, CHUNK), CHUNK ** -0.5),
        "sgu_b": 1.0 + nrm(ks[8], (DEPTH, SGU_GROUPS, CHUNK), 0.02),
        "g_attn_out": gain(ks[9], (DEPTH, ATTN_WIDTH)),
        "g_sgu_out": gain(ks[10], (DEPTH, SGU_WIDTH)),
        "w_out": nrm(ks[11], (DEPTH, MIX_WIDTH, D_MODEL), MIX_WIDTH ** -0.5),
        "g_ffn": gain(ks[12], (DEPTH, D_MODEL)),
        "w_up": nrm(ks[13], (DEPTH, D_MODEL, 2 * D_FF), D_MODEL ** -0.5),
        "conv_w": nrm(ks[14], (DEPTH, CONV_WIDTH, D_FF), CONV_WIDTH ** -0.5),
        "conv_b": nrm(ks[15], (DEPTH, D_FF), 0.02),
        "w_down": nrm(ks[16], (DEPTH, D_FF, D_MODEL), D_FF ** -0.5),
        "g_ple": gain(ks[17], (DEPTH, D_MODEL)),
        "w_ple_gate": nrm(ks[18], (DEPTH, D_MODEL, D_MODEL), D_MODEL ** -0.5),
        "w_ple_proj": nrm(ks[19], (DEPTH, PLE_DIM, D_MODEL), PLE_DIM ** -0.5),
        "g_final": gain(ks[20], (D_MODEL,)),
    }


def reference(x, p, g_mix, w_in, attn_sink, sgu_ln_g, sgu_ln_b, sgu_w, sgu_b,
              g_attn_out, g_sgu_out, w_out, g_ffn, w_up, conv_w, conv_b, w_down,
              g_ple, w_ple_gate, w_ple_proj, g_final):
    b, s, _ = x.shape
    h = x
    splits = [ATTN_WIDTH, ATTN_WIDTH + KV_WIDTH, ATTN_WIDTH + 2 * KV_WIDTH,
              ATTN_WIDTH + 2 * KV_WIDTH + SGU_WIDTH]
    for i in range(DEPTH):
        a = rmsnorm(h, g_mix[i])
        z = a @ w_in[i]
        q, k, v, zu, zv = jnp.split(z, splits, axis=-1)
        q = q.reshape(b, s, N_HEADS, HEAD_DIM)
        k = k.reshape(b, s, N_KV, HEAD_DIM)
        v = v.reshape(b, s, N_KV, HEAD_DIM)
        attn_o = windowed_gqa(q, k, v, attn_sink[i])
        sgu_o = chunked_sgu(zu, zv, sgu_ln_g[i], sgu_ln_b[i], sgu_w[i], sgu_b[i])
        merged = jnp.concatenate([rmsnorm(attn_o, g_attn_out[i]),
                                  rmsnorm(sgu_o, g_sgu_out[i])], axis=-1)
        h = h + merged @ w_out[i]
        c = rmsnorm(h, g_ffn[i])
        gate, val = jnp.split(c @ w_up[i], 2, axis=-1)
        gate = dwconv3_centred(gate, conv_w[i], conv_b[i])
        h = h + (jax.nn.gelu(gate, approximate=True) * val) @ w_down[i]
        pg = jax.nn.sigmoid(rmsnorm(h, g_ple[i]) @ w_ple_gate[i])
        h = h + (p[i] @ w_ple_proj[i]) * pg
    return rmsnorm(h, g_final)
```

```python
import functools
import math

import jax
import jax.numpy as jnp
import numpy as np
from jax import lax
from jax.experimental import pallas as pl
from jax.experimental.pallas import tpu as pltpu

D_MODEL = 1024
ATTN_WIDTH = 512
SGU_WIDTH = 512
HEAD_DIM = 64
N_HEADS = 8
N_KV = 2
GQ = 4
KV_WIDTH = N_KV * HEAD_DIM
WINDOW = 128
BLK = 128
SGU_GROUPS = 8
D_FF = 2816
PLE_DIM = 256
IN_WIDTH = ATTN_WIDTH + 2 * KV_WIDTH + 2 * SGU_WIDTH
RMS_EPS = 1e-6
LN_EPS = 1e-5
NEG_INF = -1e30

LANES = 128
SUBLANES = 8
N_PAIRS = N_HEADS // 2
ALIBI_SLOPES = tuple(2.0 ** (-8.0 * (h + 1) / N_HEADS) for h in range(N_HEADS))

MIX_TILE = 512
FFN_TILE = 512
FF_CHUNK = 256
VMEM_LIMIT_BYTES = 56 * 1024 * 1024

BF16 = jnp.bfloat16
F32 = jnp.float32


def _rms_scale(x):
    return lax.rsqrt(jnp.mean(x * x, axis=-1, keepdims=True) + RMS_EPS)


def _dot(a, b):
    return jnp.dot(a, b, preferred_element_type=F32)


def _dot_nt(a, b):
    return lax.dot_general(a, b, (((1,), (1,)), ((), ())), preferred_element_type=F32)


def _mix_kernel(sink_ref, x_ref, xp_ref, xn_ref, g_mix_ref, w_in_ref, ln_g_ref, ln_b_ref,
                wcat_ref, bst_ref, g_attn_ref, g_sgu_ref, w_out_ref, o_ref,
                q_s, kv_s, u_s, vv_s, mg_s, *, seq_len):
    ts = x_ref.shape[1]
    nblk = ts // BLK
    t_idx = pl.program_id(1)

    x = x_ref[0]
    a = (x * _rms_scale(x) * g_mix_ref[...]).astype(BF16)
    q_s[...] = (_dot(a, w_in_ref[:, 0:ATTN_WIDTH]) * (HEAD_DIM ** -0.5)).astype(BF16)
    kv_lo, kv_hi = ATTN_WIDTH, ATTN_WIDTH + 2 * KV_WIDTH
    kv_s[BLK:BLK + ts, :] = _dot(a, w_in_ref[:, kv_lo:kv_hi]).astype(BF16)
    zu = _dot(a, w_in_ref[:, kv_hi:kv_hi + SGU_WIDTH])
    u_s[...] = jax.nn.gelu(zu, approximate=True)
    zv = _dot(a, w_in_ref[:, kv_hi + SGU_WIDTH:IN_WIDTH])
    gv = jax.nn.gelu(zv, approximate=True)
    mu = jnp.mean(gv, axis=-1, keepdims=True)
    var = jnp.mean(jnp.square(gv - mu), axis=-1, keepdims=True)
    vv = (gv - mu) * lax.rsqrt(var + LN_EPS) * ln_g_ref[...] + ln_b_ref[...]
    vv_s[...] = vv.astype(BF16)

    xh = jnp.concatenate([xp_ref[0], xn_ref[0]], axis=0)
    ah = (xh * _rms_scale(xh) * g_mix_ref[...]).astype(BF16)
    kvh = _dot(ah, w_in_ref[:, kv_lo:kv_hi]).astype(BF16)
    kv_s[0:BLK, :] = kvh[0:BLK]
    kv_s[BLK + ts:BLK + ts + BLK, :] = kvh[BLK:2 * BLK]

    lane = lax.broadcasted_iota(jnp.int32, (1, LANES), 1)
    lo_lane = lane < HEAD_DIM
    qi = lax.broadcasted_iota(jnp.int32, (BLK, 3 * BLK), 0)
    kj = lax.broadcasted_iota(jnp.int32, (BLK, 3 * BLK), 1)
    dist = jnp.abs(qi + BLK - kj)
    in_window = dist <= WINDOW
    neg_dist = -dist.astype(F32)

    def block_body(i, carry):
        r0 = pl.multiple_of(i * BLK, BLK)
        gblk = t_idx * nblk + i
        key_pos = gblk * BLK - BLK + kj
        valid = in_window & (key_pos >= 0) & (key_pos < seq_len)

        kb = kv_s[pl.ds(r0, 3 * BLK), 0:KV_WIDTH]
        vb = kv_s[pl.ds(r0, 3 * BLK), KV_WIDTH:2 * KV_WIDTH]
        zero = jnp.zeros_like(kb)
        k_cat = jnp.concatenate([jnp.where(lo_lane, kb, zero), jnp.where(lo_lane, zero, kb)], axis=0)
        v_cat = jnp.concatenate([jnp.where(lo_lane, vb, zero), jnp.where(lo_lane, zero, vb)], axis=0)
        qb = q_s[pl.ds(r0, BLK), :]
        q_stack = jnp.concatenate([qb[:, g * LANES:(g + 1) * LANES] for g in range(N_PAIRS)], axis=0)
        s_all = _dot_nt(q_stack, k_cat)
        p_rows = []
        den_rows = []
        for g in range(N_PAIRS):
            p_halves = []
            den_halves = []
            for half in range(2):
                head = g + N_PAIRS * half
                s = s_all[g * BLK:(g + 1) * BLK, half * 3 * BLK:(half + 1) * 3 * BLK]
                sc = jnp.where(valid, s + ALIBI_SLOPES[head] * neg_dist, NEG_INF)
                sink = sink_ref[head]
                m = jnp.maximum(jnp.max(sc, axis=-1, keepdims=True), sink)
                e = jnp.exp(sc - m)
                denom = jnp.sum(e, axis=-1, keepdims=True) + jnp.exp(sink - m)
                p_halves.append(e.astype(BF16))
                den_halves.append(denom)
            p_rows.append(jnp.concatenate(p_halves, axis=1))
            den_rows.append(jnp.where(lo_lane, den_halves[0], den_halves[1]))
        p_all = jnp.concatenate(p_rows, axis=0)
        o_all = _dot(p_all, v_cat)
        attn = jnp.concatenate(
            [o_all[g * BLK:(g + 1) * BLK] / den_rows[g] for g in range(N_PAIRS)], axis=1)
        attn = attn * _rms_scale(attn) * g_attn_ref[...]
        mg_s[pl.ds(r0, BLK), 0:ATTN_WIDTH] = attn.astype(BF16)

        vvb = vv_s[pl.ds(r0, BLK), :]
        mixed = []
        for j in range(SGU_GROUPS // 2):
            vp = vvb[:, j * LANES:(j + 1) * LANES]
            zp = jnp.zeros_like(vp)
            rhs = jnp.concatenate([jnp.where(lo_lane, vp, zp), jnp.where(lo_lane, zp, vp)], axis=0)
            mixed.append(_dot(wcat_ref[j], rhs))
        mixed = jnp.concatenate(mixed, axis=1) + bst_ref[...]
        sgu = u_s[pl.ds(r0, BLK), :] * mixed
        sgu = sgu * _rms_scale(sgu) * g_sgu_ref[...]
        mg_s[pl.ds(r0, BLK), ATTN_WIDTH:ATTN_WIDTH + SGU_WIDTH] = sgu.astype(BF16)
        return carry

    lax.fori_loop(0, nblk, block_body, 0)

    o_ref[0] = x_ref[0] + _dot(mg_s[...], w_out_ref[...])


def _ffn_kernel(h_ref, hp_ref, hn_ref, p_ref, g_ffn_ref, w_up_ref, conv_w_ref, conv_b_ref,
                w_down_ref, g_ple_ref, w_gate_ref, w_proj_ref, g_final_ref, o_ref,
                gate_s, acc_s, *, final_norm):
    ts = h_ref.shape[1]
    t_idx = pl.program_id(1)
    n_t = pl.num_programs(1)

    h = h_ref[0]
    c = (h * _rms_scale(h) * g_ffn_ref[...]).astype(BF16)
    hh = jnp.concatenate([hp_ref[0], hn_ref[0]], axis=0)
    ch = (hh * _rms_scale(hh) * g_ffn_ref[...]).astype(BF16)
    row = lax.broadcasted_iota(jnp.int32, (2 * SUBLANES, 1), 0)
    has_prev = (t_idx > 0).astype(F32)
    has_next = (t_idx < n_t - 1).astype(F32)
    edge_keep = jnp.where(row < SUBLANES, has_prev, has_next) > 0.5

    for j in range(D_FF // FF_CHUNK):
        c0 = j * FF_CHUNK
        gate = _dot(c, w_up_ref[:, c0:c0 + FF_CHUNK])
        val = _dot(c, w_up_ref[:, D_FF + c0:D_FF + c0 + FF_CHUNK])
        gate_h = jnp.where(edge_keep, _dot(ch, w_up_ref[:, c0:c0 + FF_CHUNK]), 0.0)
        gate_s[0:SUBLANES, :] = gate_h[0:SUBLANES]
        gate_s[SUBLANES:SUBLANES + ts, :] = gate
        gate_s[SUBLANES + ts:2 * SUBLANES + ts, :] = gate_h[SUBLANES:2 * SUBLANES]
        g_prev = gate_s[SUBLANES - 1:SUBLANES - 1 + ts, :]
        g_next = gate_s[SUBLANES + 1:SUBLANES + 1 + ts, :]
        cw = conv_w_ref[:, c0:c0 + FF_CHUNK]
        conv = (g_prev * cw[0:1] + gate * cw[1:2] + g_next * cw[2:3]
                + conv_b_ref[:, c0:c0 + FF_CHUNK])
        act = (jax.nn.gelu(conv, approximate=True) * val).astype(BF16)
        down = _dot(act, w_down_ref[c0:c0 + FF_CHUNK, :])
        if j == 0:
            acc_s[...] = down
        else:
            acc_s[...] += down

    h2 = h_ref[0] + acc_s[...]
    cg = (h2 * _rms_scale(h2) * g_ple_ref[...]).astype(BF16)
    pg = jax.nn.sigmoid(_dot(cg, w_gate_ref[...]))
    h3 = h2 + _dot(p_ref[0].astype(BF16), w_proj_ref[...]) * pg
    if final_norm:
        h3 = h3 * _rms_scale(h3) * g_final_ref[...]
    o_ref[0] = h3


def _const_spec(shape):
    nd = len(shape)
    return pl.BlockSpec(shape, lambda b, t: (0,) * nd, pipeline_mode=pl.Buffered(1))


def _q_perm():
    perm = np.empty((ATTN_WIDTH,), np.int32)
    for n in range(ATTN_WIDTH):
        g, r = divmod(n, LANES)
        head = g if r < HEAD_DIM else N_PAIRS + g
        perm[n] = head * HEAD_DIM + r % HEAD_DIM
    return perm


def kernel(x, p, g_mix, w_in, attn_sink, sgu_ln_g, sgu_ln_b, sgu_w, sgu_b, g_attn_out, g_sgu_out,
           w_out, g_ffn, w_up, conv_w, conv_b, w_down, g_ple, w_ple_gate, w_ple_proj, g_final):
    depth = w_in.shape[0]
    bsz, seq, d = x.shape
    assert d == D_MODEL and seq % MIX_TILE == 0 and seq % FFN_TILE == 0
    perm = _q_perm()
    row = lambda v: v.reshape(1, -1).astype(F32)

    h = x
    for i in range(depth):
        w_in_i = jnp.concatenate([w_in[i][:, perm], w_in[i][:, ATTN_WIDTH:]], axis=1).astype(BF16)
        w_out_i = jnp.concatenate([w_out[i][perm], w_out[i][ATTN_WIDTH:]], axis=0).astype(BF16)
        g_attn_i = row(g_attn_out[i][perm])
        wcat = sgu_w[i].reshape(SGU_GROUPS // 2, 2, BLK, BLK).transpose(0, 2, 1, 3)
        wcat = wcat.reshape(SGU_GROUPS // 2, BLK, 2 * BLK).astype(BF16)
        bst = jnp.repeat(sgu_b[i].T, SGU_WIDTH // SGU_GROUPS, axis=1).astype(F32)

        nb = seq // BLK
        mix_steps = seq // MIX_TILE
        bpt = MIX_TILE // BLK
        mix = pl.pallas_call(
            functools.partial(_mix_kernel, seq_len=seq),
            out_shape=jax.ShapeDtypeStruct((bsz, seq, D_MODEL), F32),
            grid=(bsz, mix_steps),
            in_specs=[
                pl.BlockSpec(memory_space=pltpu.SMEM),
                pl.BlockSpec((1, MIX_TILE, D_MODEL), lambda b, t: (b, t, 0)),
                pl.BlockSpec((1, BLK, D_MODEL), lambda b, t: (b, jnp.maximum(t * bpt - 1, 0), 0)),
                pl.BlockSpec((1, BLK, D_MODEL), lambda b, t: (b, jnp.minimum((t + 1) * bpt, nb - 1), 0)),
                _const_spec((1, D_MODEL)),
                _const_spec((D_MODEL, IN_WIDTH)),
                _const_spec((1, SGU_WIDTH)),
                _const_spec((1, SGU_WIDTH)),
                _const_spec((SGU_GROUPS // 2, BLK, 2 * BLK)),
                _const_spec((BLK, SGU_WIDTH)),
                _const_spec((1, ATTN_WIDTH)),
                _const_spec((1, SGU_WIDTH)),
                _const_spec((D_MODEL, D_MODEL)),
            ],
            out_specs=pl.BlockSpec((1, MIX_TILE, D_MODEL), lambda b, t: (b, t, 0)),
            scratch_shapes=[
                pltpu.VMEM((MIX_TILE, ATTN_WIDTH), BF16),
                pltpu.VMEM((MIX_TILE + 2 * BLK, 2 * KV_WIDTH), BF16),
                pltpu.VMEM((MIX_TILE, SGU_WIDTH), F32),
                pltpu.VMEM((MIX_TILE, SGU_WIDTH), BF16),
                pltpu.VMEM((MIX_TILE, D_MODEL), BF16),
            ],
            compiler_params=pltpu.CompilerParams(
                dimension_semantics=("parallel", "parallel"),
                vmem_limit_bytes=VMEM_LIMIT_BYTES),
            name="mix_kernel",
        )
        h1 = mix(attn_sink[i].astype(F32), h, h, h, row(g_mix[i]), w_in_i, row(sgu_ln_g[i]),
                 row(sgu_ln_b[i]), wcat, bst, g_attn_i, row(g_sgu_out[i]), w_out_i)

        ffn_steps = seq // FFN_TILE
        spt = FFN_TILE // SUBLANES
        ns = seq // SUBLANES
        ffn = pl.pallas_call(
            functools.partial(_ffn_kernel, final_norm=(i == depth - 1)),
            out_shape=jax.ShapeDtypeStruct((bsz, seq, D_MODEL), F32),
            grid=(bsz, ffn_steps),
            in_specs=[
                pl.BlockSpec((1, FFN_TILE, D_MODEL), lambda b, t: (b, t, 0)),
                pl.BlockSpec((1, SUBLANES, D_MODEL), lambda b, t: (b, jnp.maximum(t * spt - 1, 0), 0)),
                pl.BlockSpec((1, SUBLANES, D_MODEL), lambda b, t: (b, jnp.minimum((t + 1) * spt, ns - 1), 0)),
                pl.BlockSpec((1, FFN_TILE, PLE_DIM), lambda b, t: (b, t, 0)),
                _const_spec((1, D_MODEL)),
                _const_spec((D_MODEL, 2 * D_FF)),
                _const_spec((3, D_FF)),
                _const_spec((1, D_FF)),
                _const_spec((D_FF, D_MODEL)),
                _const_spec((1, D_MODEL)),
                _const_spec((D_MODEL, D_MODEL)),
                _const_spec((PLE_DIM, D_MODEL)),
                _const_spec((1, D_MODEL)),
            ],
            out_specs=pl.BlockSpec((1, FFN_TILE, D_MODEL), lambda b, t: (b, t, 0)),
            scratch_shapes=[
                pltpu.VMEM((FFN_TILE + 2 * SUBLANES, FF_CHUNK), F32),
                pltpu.VMEM((FFN_TILE, D_MODEL), F32),
            ],
            compiler_params=pltpu.CompilerParams(
                dimension_semantics=("parallel", "parallel"),
                vmem_limit_bytes=VMEM_LIMIT_BYTES),
            name="ffn_kernel",
        )
        h = ffn(h1, h1, h1, p[i], row(g_ffn[i]), w_up[i].astype(BF16), conv_w[i].astype(F32),
                row(conv_b[i]), w_down[i].astype(BF16), row(g_ple[i]), w_ple_gate[i].astype(BF16),
                w_ple_proj[i].astype(BF16), row(g_final))
    return h
```

```python
import functools
import math

import jax
import jax.numpy as jnp
import numpy as np
from jax import lax
from jax.experimental import pallas as pl
from jax.experimental.pallas import tpu as pltpu

D_MODEL = 1024
ATTN_WIDTH = 512
SGU_WIDTH = 512
HEAD_DIM = 64
N_HEADS = 8
N_KV = 2
GQ = 4
KV_WIDTH = N_KV * HEAD_DIM
WINDOW = 128
BLK = 128
SGU_GROUPS = 8
D_FF = 2816
PLE_DIM = 256
IN_WIDTH = ATTN_WIDTH + 2 * KV_WIDTH + 2 * SGU_WIDTH
RMS_EPS = 1e-6
LN_EPS = 1e-5
NEG_INF = -1e30

LANES = 128
SUBLANES = 8
N_PAIRS = N_HEADS // 2
ALIBI_SLOPES = tuple(2.0 ** (-8.0 * (h + 1) / N_HEADS) for h in range(N_HEADS))

MIX_TILE = 512
FFN_TILE = 512
FF_CHUNK = 256
VMEM_LIMIT_BYTES = 56 * 1024 * 1024

BF16 = jnp.bfloat16
F32 = jnp.float32


def _rms_scale(x):
    return lax.rsqrt(jnp.mean(x * x, axis=-1, keepdims=True) + RMS_EPS)


def _dot(a, b):
    return jnp.dot(a, b, preferred_element_type=F32)


def _dot_nt(a, b):
    return lax.dot_general(a, b, (((1,), (1,)), ((), ())), preferred_element_type=F32)


def _mix_kernel(sink_ref, x_ref, xp_ref, xn_ref, g_mix_ref, w_in_ref, ln_g_ref, ln_b_ref,
                wcat_ref, bst_ref, g_attn_ref, g_sgu_ref, w_out_ref, o_ref,
                q_s, kv_s, u_s, vv_s, mg_s, *, seq_len):
    ts = x_ref.shape[1]
    nblk = ts // BLK
    t_idx = pl.program_id(1)

    x = x_ref[0]
    a = (x * _rms_scale(x) * g_mix_ref[...]).astype(BF16)
    q_s[...] = (_dot(a, w_in_ref[:, 0:ATTN_WIDTH]) * (HEAD_DIM ** -0.5)).astype(BF16)
    kv_lo, kv_hi = ATTN_WIDTH, ATTN_WIDTH + 2 * KV_WIDTH
    kv_s[BLK:BLK + ts, :] = _dot(a, w_in_ref[:, kv_lo:kv_hi]).astype(BF16)
    zu = _dot(a, w_in_ref[:, kv_hi:kv_hi + SGU_WIDTH])
    u_s[...] = jax.nn.gelu(zu, approximate=True)
    zv = _dot(a, w_in_ref[:, kv_hi + SGU_WIDTH:IN_WIDTH])
    gv = jax.nn.gelu(zv, approximate=True)
    mu = jnp.mean(gv, axis=-1, keepdims=True)
    var = jnp.mean(jnp.square(gv - mu), axis=-1, keepdims=True)
    vv = (gv - mu) * lax.rsqrt(var + LN_EPS) * ln_g_ref[...] + ln_b_ref[...]
    vv_s[...] = vv.astype(BF16)

    xh = jnp.concatenate([xp_ref[0], xn_ref[0]], axis=0)
    ah = (xh * _rms_scale(xh) * g_mix_ref[...]).astype(BF16)
    kvh = _dot(ah, w_in_ref[:, kv_lo:kv_hi]).astype(BF16)
    kv_s[0:BLK, :] = kvh[0:BLK]
    kv_s[BLK + ts:BLK + ts + BLK, :] = kvh[BLK:2 * BLK]

    lane = lax.broadcasted_iota(jnp.int32, (1, LANES), 1)
    lo_lane = lane < HEAD_DIM
    qi = lax.broadcasted_iota(jnp.int32, (BLK, 3 * BLK), 0)
    kj = lax.broadcasted_iota(jnp.int32, (BLK, 3 * BLK), 1)
    dist = jnp.abs(qi + BLK - kj)
    in_window = dist <= WINDOW
    neg_dist = -dist.astype(F32)

    def block_body(i, carry):
        r0 = pl.multiple_of(i * BLK, BLK)
        gblk = t_idx * nblk + i
        key_pos = gblk * BLK - BLK + kj
        valid = in_window & (key_pos >= 0) & (key_pos < seq_len)

        kb = kv_s[pl.ds(r0, 3 * BLK), 0:KV_WIDTH]
        vb = kv_s[pl.ds(r0, 3 * BLK), KV_WIDTH:2 * KV_WIDTH]
        zero = jnp.zeros_like(kb)
        k_cat = jnp.concatenate([jnp.where(lo_lane, kb, zero), jnp.where(lo_lane, zero, kb)], axis=0)
        v_cat = jnp.concatenate([jnp.where(lo_lane, vb, zero), jnp.where(lo_lane, zero, vb)], axis=0)
        qb = q_s[pl.ds(r0, BLK), :]
        q_stack = jnp.concatenate([qb[:, g * LANES:(g + 1) * LANES] for g in range(N_PAIRS)], axis=0)
        s_all = _dot_nt(q_stack, k_cat)
        p_rows = []
        den_rows = []
        for g in range(N_PAIRS):
            p_halves = []
            den_halves = []
            for half in range(2):
                head = g + N_PAIRS * half
                s = s_all[g * BLK:(g + 1) * BLK, half * 3 * BLK:(half + 1) * 3 * BLK]
                sc = jnp.where(valid, s + ALIBI_SLOPES[head] * neg_dist, NEG_INF)
                sink = sink_ref[head]
                m = jnp.maximum(jnp.max(sc, axis=-1, keepdims=True), sink)
                e = jnp.exp(sc - m)
                denom = jnp.sum(e, axis=-1, keepdims=True) + jnp.exp(sink - m)
                p_halves.append(e.astype(BF16))
                den_halves.append(denom)
            p_rows.append(jnp.concatenate(p_halves, axis=1))
            den_rows.append(jnp.where(lo_lane, den_halves[0], den_halves[1]))
        p_all = jnp.concatenate(p_rows, axis=0)
        o_all = _dot(p_all, v_cat)
        attn = jnp.concatenate(
            [o_all[g * BLK:(g + 1) * BLK] / den_rows[g] for g in range(N_PAIRS)], axis=1)
        attn = attn * _rms_scale(attn) * g_attn_ref[...]
        mg_s[pl.ds(r0, BLK), 0:ATTN_WIDTH] = attn.astype(BF16)

        vvb = vv_s[pl.ds(r0, BLK), :]
        mixed = []
        for j in range(SGU_GROUPS // 2):
            vp = vvb[:, j * LANES:(j + 1) * LANES]
            zp = jnp.zeros_like(vp)
            rhs = jnp.concatenate([jnp.where(lo_lane, vp, zp), jnp.where(lo_lane, zp, vp)], axis=0)
            mixed.append(_dot(wcat_ref[j], rhs))
        mixed = jnp.concatenate(mixed, axis=1) + bst_ref[...]
        sgu = u_s[pl.ds(r0, BLK), :] * mixed
        sgu = sgu * _rms_scale(sgu) * g_sgu_ref[...]
        mg_s[pl.ds(r0, BLK), ATTN_WIDTH:ATTN_WIDTH + SGU_WIDTH] = sgu.astype(BF16)
        return carry

    lax.fori_loop(0, nblk, block_body, 0)

    o_ref[0] = x_ref[0] + _dot(mg_s[...], w_out_ref[...])


def _ffn_kernel(h_ref, hp_ref, hn_ref, p_ref, g_ffn_ref, w_up_ref, conv_w_ref, conv_b_ref,
                w_down_ref, g_ple_ref, w_gate_ref, w_proj_ref, g_final_ref, o_ref,
                act_s, *, final_norm):
    ts = h_ref.shape[1]
    t_idx = pl.program_id(1)
    n_t = pl.num_programs(1)

    h = h_ref[0]
    c = (h * _rms_scale(h) * g_ffn_ref[...]).astype(BF16)
    hh = jnp.concatenate([hp_ref[0], hn_ref[0]], axis=0)
    ch = (hh * _rms_scale(hh) * g_ffn_ref[...]).astype(BF16)
    c_ext = jnp.concatenate([c, ch], axis=0)
    row = lax.broadcasted_iota(jnp.int32, (2 * SUBLANES, 1), 0)
    has_prev = (t_idx > 0).astype(F32)
    has_next = (t_idx < n_t - 1).astype(F32)
    edge_keep = jnp.where(row < SUBLANES, has_prev, has_next) > 0.5
    row8 = lax.broadcasted_iota(jnp.int32, (SUBLANES, 1), 0)

    def up_proj(j):
        c0 = j * FF_CHUNK
        gate_all = _dot(c_ext, w_up_ref[:, c0:c0 + FF_CHUNK])
        val = _dot(c, w_up_ref[:, D_FF + c0:D_FF + c0 + FF_CHUNK])
        gate_h = jnp.where(edge_keep, gate_all[ts:ts + 2 * SUBLANES], 0.0)
        return gate_all[0:ts], val, gate_h

    n_chunks = D_FF // FF_CHUNK
    for j in range(n_chunks):
        c0 = j * FF_CHUNK
        gate, val, gate_h = up_proj(j)
        g_prev = pltpu.roll(gate, 1, axis=0)
        g_next = pltpu.roll(gate, ts - 1, axis=0)
        first = jnp.where(row8 == 0, gate_h[SUBLANES - 1:SUBLANES], g_prev[0:SUBLANES])
        last = jnp.where(row8 == SUBLANES - 1, gate_h[SUBLANES:SUBLANES + 1], g_next[ts - SUBLANES:ts])
        g_prev = jnp.concatenate([first, g_prev[SUBLANES:]], axis=0)
        g_next = jnp.concatenate([g_next[:ts - SUBLANES], last], axis=0)
        cw = conv_w_ref[:, c0:c0 + FF_CHUNK]
        conv = (g_prev * cw[0:1] + gate * cw[1:2] + g_next * cw[2:3]
                + conv_b_ref[:, c0:c0 + FF_CHUNK])
        act_s[:, c0:c0 + FF_CHUNK] = (jax.nn.gelu(conv, approximate=True) * val).astype(BF16)

    h2 = h_ref[0] + _dot(act_s[...], w_down_ref[...])
    cg = (h2 * _rms_scale(h2) * g_ple_ref[...]).astype(BF16)
    pg = jax.nn.sigmoid(_dot(cg, w_gate_ref[...]))
    h3 = h2 + _dot(p_ref[0].astype(BF16), w_proj_ref[...]) * pg
    if final_norm:
        h3 = h3 * _rms_scale(h3) * g_final_ref[...]
    o_ref[0] = h3


def _const_spec(shape):
    nd = len(shape)
    return pl.BlockSpec(shape, lambda b, t: (0,) * nd, pipeline_mode=pl.Buffered(1))


def _q_perm():
    perm = np.empty((ATTN_WIDTH,), np.int32)
    for n in range(ATTN_WIDTH):
        g, r = divmod(n, LANES)
        head = g if r < HEAD_DIM else N_PAIRS + g
        perm[n] = head * HEAD_DIM + r % HEAD_DIM
    return perm


def kernel(x, p, g_mix, w_in, attn_sink, sgu_ln_g, sgu_ln_b, sgu_w, sgu_b, g_attn_out, g_sgu_out,
           w_out, g_ffn, w_up, conv_w, conv_b, w_down, g_ple, w_ple_gate, w_ple_proj, g_final):
    depth = w_in.shape[0]
    bsz, seq, d = x.shape
    assert d == D_MODEL and seq % MIX_TILE == 0 and seq % FFN_TILE == 0
    perm = _q_perm()
    row = lambda v: v.reshape(1, -1).astype(F32)

    h = x
    for i in range(depth):
        w_in_i = jnp.concatenate([w_in[i][:, perm], w_in[i][:, ATTN_WIDTH:]], axis=1).astype(BF16)
        w_out_i = jnp.concatenate([w_out[i][perm], w_out[i][ATTN_WIDTH:]], axis=0).astype(BF16)
        g_attn_i = row(g_attn_out[i][perm])
        wcat = sgu_w[i].reshape(SGU_GROUPS // 2, 2, BLK, BLK).transpose(0, 2, 1, 3)
        wcat = wcat.reshape(SGU_GROUPS // 2, BLK, 2 * BLK).astype(BF16)
        bst = jnp.repeat(sgu_b[i].T, SGU_WIDTH // SGU_GROUPS, axis=1).astype(F32)

        nb = seq // BLK
        mix_steps = seq // MIX_TILE
        bpt = MIX_TILE // BLK
        mix = pl.pallas_call(
            functools.partial(_mix_kernel, seq_len=seq),
            out_shape=jax.ShapeDtypeStruct((bsz, seq, D_MODEL), F32),
            grid=(bsz, mix_steps),
            in_specs=[
                pl.BlockSpec(memory_space=pltpu.SMEM),
                pl.BlockSpec((1, MIX_TILE, D_MODEL), lambda b, t: (b, t, 0)),
                pl.BlockSpec((1, BLK, D_MODEL), lambda b, t: (b, jnp.maximum(t * bpt - 1, 0), 0)),
                pl.BlockSpec((1, BLK, D_MODEL), lambda b, t: (b, jnp.minimum((t + 1) * bpt, nb - 1), 0)),
                _const_spec((1, D_MODEL)),
                _const_spec((D_MODEL, IN_WIDTH)),
                _const_spec((1, SGU_WIDTH)),
                _const_spec((1, SGU_WIDTH)),
                _const_spec((SGU_GROUPS // 2, BLK, 2 * BLK)),
                _const_spec((BLK, SGU_WIDTH)),
                _const_spec((1, ATTN_WIDTH)),
                _const_spec((1, SGU_WIDTH)),
                _const_spec((D_MODEL, D_MODEL)),
            ],
            out_specs=pl.BlockSpec((1, MIX_TILE, D_MODEL), lambda b, t: (b, t, 0)),
            scratch_shapes=[
                pltpu.VMEM((MIX_TILE, ATTN_WIDTH), BF16),
                pltpu.VMEM((MIX_TILE + 2 * BLK, 2 * KV_WIDTH), BF16),
                pltpu.VMEM((MIX_TILE, SGU_WIDTH), F32),
                pltpu.VMEM((MIX_TILE, SGU_WIDTH), BF16),
                pltpu.VMEM((MIX_TILE, D_MODEL), BF16),
            ],
            compiler_params=pltpu.CompilerParams(
                dimension_semantics=("parallel", "parallel"),
                vmem_limit_bytes=VMEM_LIMIT_BYTES),
            name="mix_kernel",
        )
        h1 = mix(attn_sink[i].astype(F32), h, h, h, row(g_mix[i]), w_in_i, row(sgu_ln_g[i]),
                 row(sgu_ln_b[i]), wcat, bst, g_attn_i, row(g_sgu_out[i]), w_out_i)

        ffn_steps = seq // FFN_TILE
        spt = FFN_TILE // SUBLANES
        ns = seq // SUBLANES
        ffn = pl.pallas_call(
            functools.partial(_ffn_kernel, final_norm=(i == depth - 1)),
            out_shape=jax.ShapeDtypeStruct((bsz, seq, D_MODEL), F32),
            grid=(bsz, ffn_steps),
            in_specs=[
                pl.BlockSpec((1, FFN_TILE, D_MODEL), lambda b, t: (b, t, 0)),
                pl.BlockSpec((1, SUBLANES, D_MODEL), lambda b, t: (b, jnp.maximum(t * spt - 1, 0), 0)),
                pl.BlockSpec((1, SUBLANES, D_MODEL), lambda b, t: (b, jnp.minimum((t + 1) * spt, ns - 1), 0)),
                pl.BlockSpec((1, FFN_TILE, PLE_DIM), lambda b, t: (b, t, 0)),
                _const_spec((1, D_MODEL)),
                _const_spec((D_MODEL, 2 * D_FF)),
                _const_spec((3, D_FF)),
                _const_spec((1, D_FF)),
                _const_spec((D_FF, D_MODEL)),
                _const_spec((1, D_MODEL)),
                _const_spec((D_MODEL, D_MODEL)),
                _const_spec((PLE_DIM, D_MODEL)),
                _const_spec((1, D_MODEL)),
            ],
            out_specs=pl.BlockSpec((1, FFN_TILE, D_MODEL), lambda b, t: (b, t, 0)),
            scratch_shapes=[
                pltpu.VMEM((FFN_TILE, D_FF), BF16),
            ],
            compiler_params=pltpu.CompilerParams(
                dimension_semantics=("parallel", "parallel"),
                vmem_limit_bytes=VMEM_LIMIT_BYTES),
            name="ffn_kernel",
        )
        h = ffn(h1, h1, h1, p[i], row(g_ffn[i]), w_up[i].astype(BF16), conv_w[i].astype(F32),
                row(conv_b[i]), w_down[i].astype(BF16), row(g_ple[i]), w_ple_gate[i].astype(BF16),
                w_ple_proj[i].astype(BF16), row(g_final))
    return h
```

```python
import functools
import math

import jax
import jax.numpy as jnp
import numpy as np
from jax import lax
from jax.experimental import pallas as pl
from jax.experimental.pallas import tpu as pltpu

D_MODEL = 1024
ATTN_WIDTH = 512
SGU_WIDTH = 512
HEAD_DIM = 64
N_HEADS = 8
N_KV = 2
GQ = 4
KV_WIDTH = N_KV * HEAD_DIM
WINDOW = 128
BLK = 128
SGU_GROUPS = 8
D_FF = 2816
PLE_DIM = 256
IN_WIDTH = ATTN_WIDTH + 2 * KV_WIDTH + 2 * SGU_WIDTH
RMS_EPS = 1e-6
LN_EPS = 1e-5
NEG_INF = -1e30

LANES = 128
SUBLANES = 8
N_PAIRS = N_HEADS // 2
ALIBI_SLOPES = tuple(2.0 ** (-8.0 * (h + 1) / N_HEADS) for h in range(N_HEADS))
LOG2E = math.log2(math.e)
Q_SCALE = HEAD_DIM ** -0.5 * LOG2E
N_MASK_VARIANTS = 3

MIX_TILE = 512
FFN_TILE = 512
FF_CHUNK = 256
VMEM_LIMIT_BYTES = 56 * 1024 * 1024

BF16 = jnp.bfloat16
F32 = jnp.float32


def _rms_scale(x):
    return lax.rsqrt(jnp.mean(x * x, axis=-1, keepdims=True) + RMS_EPS)


def _dot(a, b):
    return jnp.dot(a, b, preferred_element_type=F32)


def _dot_nt(a, b):
    return lax.dot_general(a, b, (((1,), (1,)), ((), ())), preferred_element_type=F32)


def _mix_kernel(sink_ref, x_ref, xp_ref, xn_ref, g_mix_ref, w_in_ref, ln_g_ref, ln_b_ref,
                wcat_ref, bst_ref, g_attn_ref, g_sgu_ref, w_out_ref, bias_ref, keep_ref, o_ref,
                q_s, kv_s, u_s, vv_s, mg_s):
    ts = x_ref.shape[1]
    nblk = ts // BLK
    t_idx = pl.program_id(1)
    n_t = pl.num_programs(1)
    lane = lax.broadcasted_iota(jnp.int32, (1, LANES), 1)
    lo_lane = lane < HEAD_DIM

    def split_lanes(v):
        zero = jnp.zeros_like(v)
        return jnp.where(lo_lane, v, zero), jnp.where(lo_lane, zero, v)

    x = x_ref[0]
    a = (x * _rms_scale(x) * g_mix_ref[...]).astype(BF16)
    q_s[...] = (_dot(a, w_in_ref[:, 0:ATTN_WIDTH]) * Q_SCALE).astype(BF16)
    kv_lo, kv_hi = ATTN_WIDTH, ATTN_WIDTH + 2 * KV_WIDTH

    def store_kv(r0, rows, kv):
        k0, k1 = split_lanes(kv[:, 0:KV_WIDTH])
        v0, v1 = split_lanes(kv[:, KV_WIDTH:2 * KV_WIDTH])
        kv_s[r0:r0 + rows, :] = jnp.concatenate([k0, k1, v0, v1], axis=1)

    store_kv(BLK, ts, _dot(a, w_in_ref[:, kv_lo:kv_hi]).astype(BF16))
    zu = _dot(a, w_in_ref[:, kv_hi:kv_hi + SGU_WIDTH])
    u_s[...] = jax.nn.gelu(zu, approximate=True)
    zv = _dot(a, w_in_ref[:, kv_hi + SGU_WIDTH:IN_WIDTH])
    gv = jax.nn.gelu(zv, approximate=True)
    mu = jnp.mean(gv, axis=-1, keepdims=True)
    var = jnp.mean(jnp.square(gv - mu), axis=-1, keepdims=True)
    vv = ((gv - mu) * lax.rsqrt(var + LN_EPS) * ln_g_ref[...] + ln_b_ref[...]).astype(BF16)
    for j in range(SGU_GROUPS // 2):
        lo, hi = split_lanes(vv[:, j * LANES:(j + 1) * LANES])
        vv_s[:, 2 * j * LANES:(2 * j + 1) * LANES] = lo
        vv_s[:, (2 * j + 1) * LANES:(2 * j + 2) * LANES] = hi

    xh = jnp.concatenate([xp_ref[0], xn_ref[0]], axis=0)
    ah = (xh * _rms_scale(xh) * g_mix_ref[...]).astype(BF16)
    kvh = _dot(ah, w_in_ref[:, kv_lo:kv_hi]).astype(BF16)
    store_kv(0, BLK, kvh[0:BLK])
    store_kv(BLK + ts, BLK, kvh[BLK:2 * BLK])

    lo_f = jnp.where(lo_lane, 1.0, 0.0)
    ones_lo = jnp.broadcast_to(lo_f.astype(BF16), (3 * BLK, LANES))
    ones_hi = jnp.broadcast_to((1.0 - lo_f).astype(BF16), (3 * BLK, LANES))
    sink_pairs = [jnp.where(lo_lane, sink_ref[g], sink_ref[N_PAIRS + g]) * LOG2E
                  for g in range(N_PAIRS)]

    def qk(i):
        rows = slice(i * BLK, i * BLK + 3 * BLK)
        k_cat = jnp.concatenate([kv_s[rows, 0:LANES], kv_s[rows, LANES:2 * LANES]], axis=0)
        qb = q_s[i * BLK:(i + 1) * BLK, :]
        q_stack = jnp.concatenate([qb[:, g * LANES:(g + 1) * LANES] for g in range(N_PAIRS)], axis=0)
        return _dot_nt(q_stack, k_cat)

    def softmax(i, s_all):
        if i == 0:
            variant = jnp.where(t_idx == 0, 1, 0)
        elif i == nblk - 1:
            variant = jnp.where(t_idx == n_t - 1, 2, 0)
        else:
            variant = 0
        keep = keep_ref[variant]
        p_rows, m_rows = [], []
        for g in range(N_PAIRS):
            p_halves, m_halves = [], []
            for half in range(2):
                head = g + N_PAIRS * half
                s = s_all[g * BLK:(g + 1) * BLK, half * 3 * BLK:(half + 1) * 3 * BLK]
                sc = s * keep + bias_ref[variant * N_HEADS + head]
                m = jnp.maximum(jnp.max(sc, axis=-1, keepdims=True), sink_ref[head] * LOG2E)
                p_halves.append(jnp.exp2(sc - m).astype(BF16))
                m_halves.append(m)
            p_rows.append(jnp.concatenate(p_halves, axis=1))
            m_rows.append(jnp.where(lo_lane, m_halves[0], m_halves[1]))
        return jnp.concatenate(p_rows, axis=0), m_rows

    def pv(i, p_all):
        rows = slice(i * BLK, i * BLK + 3 * BLK)
        v_ext = jnp.concatenate([
            jnp.concatenate([kv_s[rows, 2 * LANES:3 * LANES], ones_lo], axis=1),
            jnp.concatenate([kv_s[rows, 3 * LANES:4 * LANES], ones_hi], axis=1)], axis=0)
        return _dot(p_all, v_ext)

    def finish_attn(i, o_ext, m_rows):
        pairs = []
        for g in range(N_PAIRS):
            og = o_ext[g * BLK:(g + 1) * BLK]
            den = og[:, LANES:2 * LANES] + jnp.exp2(sink_pairs[g] - m_rows[g])
            pairs.append(og[:, 0:LANES] / den)
        attn = jnp.concatenate(pairs, axis=1)
        attn = attn * _rms_scale(attn) * g_attn_ref[...]
        mg_s[i * BLK:(i + 1) * BLK, 0:ATTN_WIDTH] = attn.astype(BF16)

    def sgu_dots(i):
        blk = slice(i * BLK, (i + 1) * BLK)
        mixed = []
        for j in range(SGU_GROUPS // 2):
            rhs = jnp.concatenate([vv_s[blk, 2 * j * LANES:(2 * j + 1) * LANES],
                                   vv_s[blk, (2 * j + 1) * LANES:(2 * j + 2) * LANES]], axis=0)
            mixed.append(_dot(wcat_ref[j], rhs))
        return jnp.concatenate(mixed, axis=1)

    def finish_sgu(i, mixed):
        sgu = u_s[i * BLK:(i + 1) * BLK, :] * (mixed + bst_ref[...])
        sgu = sgu * _rms_scale(sgu) * g_sgu_ref[...]
        mg_s[i * BLK:(i + 1) * BLK, ATTN_WIDTH:ATTN_WIDTH + SGU_WIDTH] = sgu.astype(BF16)

    s_next = qk(0)
    for i in range(nblk):
        s_cur = s_next
        if i + 1 < nblk:
            s_next = qk(i + 1)
        p_all, m_rows = softmax(i, s_cur)
        mixed = sgu_dots(i)
        o_ext = pv(i, p_all)
        finish_sgu(i, mixed)
        finish_attn(i, o_ext, m_rows)

    o_ref[0] = x_ref[0] + _dot(mg_s[...], w_out_ref[...])


def _ffn_kernel(h_ref, hp_ref, hn_ref, p_ref, g_ffn_ref, w_up_ref, conv_w_ref, conv_b_ref,
                w_down_ref, g_ple_ref, w_gate_ref, w_proj_ref, g_final_ref, o_ref,
                act_s, *, final_norm):
    ts = h_ref.shape[1]
    t_idx = pl.program_id(1)
    n_t = pl.num_programs(1)

    h = h_ref[0]
    c = (h * _rms_scale(h) * g_ffn_ref[...]).astype(BF16)
    hh = jnp.concatenate([hp_ref[0], hn_ref[0]], axis=0)
    ch = (hh * _rms_scale(hh) * g_ffn_ref[...]).astype(BF16)
    c_ext = jnp.concatenate([c, ch], axis=0)
    row = lax.broadcasted_iota(jnp.int32, (2 * SUBLANES, 1), 0)
    has_prev = (t_idx > 0).astype(F32)
    has_next = (t_idx < n_t - 1).astype(F32)
    edge_keep = jnp.where(row < SUBLANES, has_prev, has_next) > 0.5
    row8 = lax.broadcasted_iota(jnp.int32, (SUBLANES, 1), 0)

    def up_proj(j):
        c0 = j * FF_CHUNK
        gate_all = _dot(c_ext, w_up_ref[:, c0:c0 + FF_CHUNK])
        val = _dot(c, w_up_ref[:, D_FF + c0:D_FF + c0 + FF_CHUNK])
        gate_h = jnp.where(edge_keep, gate_all[ts:ts + 2 * SUBLANES], 0.0)
        return gate_all[0:ts], val, gate_h

    n_chunks = D_FF // FF_CHUNK
    for j in range(n_chunks):
        c0 = j * FF_CHUNK
        gate, val, gate_h = up_proj(j)
        g_prev = pltpu.roll(gate, 1, axis=0)
        g_next = pltpu.roll(gate, ts - 1, axis=0)
        first = jnp.where(row8 == 0, gate_h[SUBLANES - 1:SUBLANES], g_prev[0:SUBLANES])
        last = jnp.where(row8 == SUBLANES - 1, gate_h[SUBLANES:SUBLANES + 1], g_next[ts - SUBLANES:ts])
        g_prev = jnp.concatenate([first, g_prev[SUBLANES:]], axis=0)
        g_next = jnp.concatenate([g_next[:ts - SUBLANES], last], axis=0)
        cw = conv_w_ref[:, c0:c0 + FF_CHUNK]
        conv = (g_prev * cw[0:1] + gate * cw[1:2] + g_next * cw[2:3]
                + conv_b_ref[:, c0:c0 + FF_CHUNK])
        act_s[:, c0:c0 + FF_CHUNK] = (jax.nn.gelu(conv, approximate=True) * val).astype(BF16)

    h2 = h_ref[0] + _dot(act_s[...], w_down_ref[...])
    cg = (h2 * _rms_scale(h2) * g_ple_ref[...]).astype(BF16)
    pg = jax.nn.sigmoid(_dot(cg, w_gate_ref[...]))
    h3 = h2 + _dot(p_ref[0].astype(BF16), w_proj_ref[...]) * pg
    if final_norm:
        h3 = h3 * _rms_scale(h3) * g_final_ref[...]
    o_ref[0] = h3


def _const_spec(shape):
    nd = len(shape)
    return pl.BlockSpec(shape, lambda b, t: (0,) * nd, pipeline_mode=pl.Buffered(1))


def _q_perm():
    perm = np.empty((ATTN_WIDTH,), np.int32)
    for n in range(ATTN_WIDTH):
        g, r = divmod(n, LANES)
        head = g if r < HEAD_DIM else N_PAIRS + g
        perm[n] = head * HEAD_DIM + r % HEAD_DIM
    return perm


def _mask_tables():
    qi = np.arange(BLK)[:, None]
    kj = np.arange(3 * BLK)[None, :]
    dist = np.abs(qi + BLK - kj)
    in_window = dist <= WINDOW
    key_ok = [np.ones_like(in_window), kj >= BLK, kj < 2 * BLK]
    bias = np.empty((N_MASK_VARIANTS, N_HEADS, BLK, 3 * BLK), np.float32)
    keep = np.empty((N_MASK_VARIANTS, BLK, 3 * BLK), np.float32)
    for v in range(N_MASK_VARIANTS):
        valid = in_window & key_ok[v]
        keep[v] = valid
        for h in range(N_HEADS):
            bias[v, h] = np.where(valid, -ALIBI_SLOPES[h] * dist * LOG2E, NEG_INF)
    return bias.reshape(N_MASK_VARIANTS * N_HEADS, BLK, 3 * BLK), keep


def kernel(x, p, g_mix, w_in, attn_sink, sgu_ln_g, sgu_ln_b, sgu_w, sgu_b, g_attn_out, g_sgu_out,
           w_out, g_ffn, w_up, conv_w, conv_b, w_down, g_ple, w_ple_gate, w_ple_proj, g_final):
    depth = w_in.shape[0]
    bsz, seq, d = x.shape
    assert d == D_MODEL and seq % MIX_TILE == 0 and seq % FFN_TILE == 0
    assert MIX_TILE // BLK >= 2 and seq // BLK >= 2
    perm = _q_perm()
    bias_tbl, keep_tbl = _mask_tables()
    row = lambda v: v.reshape(1, -1).astype(F32)

    h = x
    for i in range(depth):
        w_in_i = jnp.concatenate([w_in[i][:, perm], w_in[i][:, ATTN_WIDTH:]], axis=1).astype(BF16)
        w_out_i = jnp.concatenate([w_out[i][perm], w_out[i][ATTN_WIDTH:]], axis=0).astype(BF16)
        g_attn_i = row(g_attn_out[i][perm])
        wcat = sgu_w[i].reshape(SGU_GROUPS // 2, 2, BLK, BLK).transpose(0, 2, 1, 3)
        wcat = wcat.reshape(SGU_GROUPS // 2, BLK, 2 * BLK).astype(BF16)
        bst = jnp.repeat(sgu_b[i].T, SGU_WIDTH // SGU_GROUPS, axis=1).astype(F32)

        nb = seq // BLK
        mix_steps = seq // MIX_TILE
        bpt = MIX_TILE // BLK
        mix = pl.pallas_call(
            _mix_kernel,
            out_shape=jax.ShapeDtypeStruct((bsz, seq, D_MODEL), F32),
            grid=(bsz, mix_steps),
            in_specs=[
                pl.BlockSpec(memory_space=pltpu.SMEM),
                pl.BlockSpec((1, MIX_TILE, D_MODEL), lambda b, t: (b, t, 0)),
                pl.BlockSpec((1, BLK, D_MODEL), lambda b, t: (b, jnp.maximum(t * bpt - 1, 0), 0)),
                pl.BlockSpec((1, BLK, D_MODEL), lambda b, t: (b, jnp.minimum((t + 1) * bpt, nb - 1), 0)),
                _const_spec((1, D_MODEL)),
                _const_spec((D_MODEL, IN_WIDTH)),
                _const_spec((1, SGU_WIDTH)),
                _const_spec((1, SGU_WIDTH)),
                _const_spec((SGU_GROUPS // 2, BLK, 2 * BLK)),
                _const_spec((BLK, SGU_WIDTH)),
                _const_spec((1, ATTN_WIDTH)),
                _const_spec((1, SGU_WIDTH)),
                _const_spec((D_MODEL, D_MODEL)),
                _const_spec((N_MASK_VARIANTS * N_HEADS, BLK, 3 * BLK)),
                _const_spec((N_MASK_VARIANTS, BLK, 3 * BLK)),
            ],
            out_specs=pl.BlockSpec((1, MIX_TILE, D_MODEL), lambda b, t: (b, t, 0)),
            scratch_shapes=[
                pltpu.VMEM((MIX_TILE, ATTN_WIDTH), BF16),
                pltpu.VMEM((MIX_TILE + 2 * BLK, 4 * LANES), BF16),
                pltpu.VMEM((MIX_TILE, SGU_WIDTH), F32),
                pltpu.VMEM((MIX_TILE, 2 * SGU_WIDTH), BF16),
                pltpu.VMEM((MIX_TILE, D_MODEL), BF16),
            ],
            compiler_params=pltpu.CompilerParams(
                dimension_semantics=("parallel", "parallel"),
                vmem_limit_bytes=VMEM_LIMIT_BYTES),
            name="mix_kernel",
        )
        h1 = mix(attn_sink[i].astype(F32), h, h, h, row(g_mix[i]), w_in_i, row(sgu_ln_g[i]),
                 row(sgu_ln_b[i]), wcat, bst, g_attn_i, row(g_sgu_out[i]), w_out_i,
                 bias_tbl, keep_tbl)

        ffn_steps = seq // FFN_TILE
        spt = FFN_TILE // SUBLANES
        ns = seq // SUBLANES
        ffn = pl.pallas_call(
            functools.partial(_ffn_kernel, final_norm=(i == depth - 1)),
            out_shape=jax.ShapeDtypeStruct((bsz, seq, D_MODEL), F32),
            grid=(bsz, ffn_steps),
            in_specs=[
                pl.BlockSpec((1, FFN_TILE, D_MODEL), lambda b, t: (b, t, 0)),
                pl.BlockSpec((1, SUBLANES, D_MODEL), lambda b, t: (b, jnp.maximum(t * spt - 1, 0), 0)),
                pl.BlockSpec((1, SUBLANES, D_MODEL), lambda b, t: (b, jnp.minimum((t + 1) * spt, ns - 1), 0)),
                pl.BlockSpec((1, FFN_TILE, PLE_DIM), lambda b, t: (b, t, 0)),
                _const_spec((1, D_MODEL)),
                _const_spec((D_MODEL, 2 * D_FF)),
                _const_spec((3, D_FF)),
                _const_spec((1, D_FF)),
                _const_spec((D_FF, D_MODEL)),
                _const_spec((1, D_MODEL)),
                _const_spec((D_MODEL, D_MODEL)),
                _const_spec((PLE_DIM, D_MODEL)),
                _const_spec((1, D_MODEL)),
            ],
            out_specs=pl.BlockSpec((1, FFN_TILE, D_MODEL), lambda b, t: (b, t, 0)),
            scratch_shapes=[
                pltpu.VMEM((FFN_TILE, D_FF), BF16),
            ],
            compiler_params=pltpu.CompilerParams(
                dimension_semantics=("parallel", "parallel"),
                vmem_limit_bytes=VMEM_LIMIT_BYTES),
            name="ffn_kernel",
        )
        h = ffn(h1, h1, h1, p[i], row(g_ffn[i]), w_up[i].astype(BF16), conv_w[i].astype(F32),
                row(conv_b[i]), w_down[i].astype(BF16), row(g_ple[i]), w_ple_gate[i].astype(BF16),
                w_ple_proj[i].astype(BF16), row(g_final))
    return h
```

```python
import functools
import math

import jax
import jax.numpy as jnp
import numpy as np
from jax import lax
from jax.experimental import pallas as pl
from jax.experimental.pallas import tpu as pltpu

D_MODEL = 1024
ATTN_WIDTH = 512
SGU_WIDTH = 512
HEAD_DIM = 64
N_HEADS = 8
N_KV = 2
GQ = 4
KV_WIDTH = N_KV * HEAD_DIM
WINDOW = 128
BLK = 128
SGU_GROUPS = 8
D_FF = 2816
PLE_DIM = 256
IN_WIDTH = ATTN_WIDTH + 2 * KV_WIDTH + 2 * SGU_WIDTH
RMS_EPS = 1e-6
LN_EPS = 1e-5
NEG_INF = -1e30

LANES = 128
SUBLANES = 8
N_PAIRS = N_HEADS // 2
ALIBI_SLOPES = tuple(2.0 ** (-8.0 * (h + 1) / N_HEADS) for h in range(N_HEADS))
LOG2E = math.log2(math.e)
Q_SCALE = HEAD_DIM ** -0.5 * LOG2E
N_MASK_VARIANTS = 3

MIX_TILE = 512
FFN_TILE = 512
FF_CHUNK = 256
VMEM_LIMIT_BYTES = 56 * 1024 * 1024

BF16 = jnp.bfloat16
F32 = jnp.float32


def _rms_scale(x):
    return lax.rsqrt(jnp.mean(x * x, axis=-1, keepdims=True) + RMS_EPS)


def _dot(a, b):
    return jnp.dot(a, b, preferred_element_type=F32)


def _dot_nt(a, b):
    return lax.dot_general(a, b, (((1,), (1,)), ((), ())), preferred_element_type=F32)


def _mix_kernel(sink_ref, x_ref, xp_ref, xn_ref, g_mix_ref, w_in_ref, ln_g_ref, ln_b_ref,
                wcat_ref, bst_ref, g_attn_ref, g_sgu_ref, w_out_ref, bias_ref, keep_ref, o_ref,
                q_s, kv_s, at_s):
    ts = x_ref.shape[1]
    nblk = ts // BLK
    t_idx = pl.program_id(1)
    n_t = pl.num_programs(1)
    lane = lax.broadcasted_iota(jnp.int32, (1, LANES), 1)
    lo_lane = lane < HEAD_DIM

    def split_lanes(v):
        zero = jnp.zeros_like(v)
        return jnp.where(lo_lane, v, zero), jnp.where(lo_lane, zero, v)

    kv_lo, kv_hi = 2 * SGU_WIDTH + ATTN_WIDTH, IN_WIDTH

    def store_kv(r0, rows, kv):
        k0, k1 = split_lanes(kv[:, 0:KV_WIDTH])
        v0, v1 = split_lanes(kv[:, KV_WIDTH:2 * KV_WIDTH])
        kv_s[r0:r0 + rows, :] = jnp.concatenate([k0, k1, v0, v1], axis=1)

    x = x_ref[0]
    a = (x * _rms_scale(x) * g_mix_ref[...]).astype(BF16)
    z_sgu = _dot(a, w_in_ref[:, 0:2 * SGU_WIDTH])
    zv = z_sgu[:, 0:SGU_WIDTH]
    zu = z_sgu[:, SGU_WIDTH:2 * SGU_WIDTH]

    z_att = _dot(a, w_in_ref[:, 2 * SGU_WIDTH:IN_WIDTH])
    q_s[...] = (z_att[:, 0:ATTN_WIDTH] * Q_SCALE).astype(BF16)
    store_kv(BLK, ts, z_att[:, ATTN_WIDTH:ATTN_WIDTH + 2 * KV_WIDTH].astype(BF16))
    xh = jnp.concatenate([xp_ref[0], xn_ref[0]], axis=0)
    ah = (xh * _rms_scale(xh) * g_mix_ref[...]).astype(BF16)
    kvh = _dot(ah, w_in_ref[:, kv_lo:kv_hi]).astype(BF16)
    store_kv(0, BLK, kvh[0:BLK])
    store_kv(BLK + ts, BLK, kvh[BLK:2 * BLK])

    gv = jax.nn.gelu(zv, approximate=True)
    mu = jnp.mean(gv, axis=-1, keepdims=True)
    var = jnp.mean(jnp.square(gv - mu), axis=-1, keepdims=True)
    vv = ((gv - mu) * lax.rsqrt(var + LN_EPS) * ln_g_ref[...] + ln_b_ref[...]).astype(BF16)
    mixed_pairs = []
    for j in range(SGU_GROUPS // 2):
        rhs = []
        for i in range(nblk):
            lo, hi = split_lanes(vv[i * BLK:(i + 1) * BLK, j * LANES:(j + 1) * LANES])
            rhs.append(jnp.concatenate([lo, hi], axis=0))
        mixed_pairs.append(_dot(wcat_ref[j], jnp.concatenate(rhs, axis=1)))

    def gmlp_out():
        mixed_blocks = [
            jnp.concatenate([mp[:, i * LANES:(i + 1) * LANES] for mp in mixed_pairs], axis=1)
            + bst_ref[...] for i in range(nblk)]
        sgu = jax.nn.gelu(zu, approximate=True) * jnp.concatenate(mixed_blocks, axis=0)
        return (sgu * _rms_scale(sgu) * g_sgu_ref[...]).astype(BF16)

    lo_f = jnp.where(lo_lane, 1.0, 0.0)
    ones_lo = jnp.broadcast_to(lo_f.astype(BF16), (3 * BLK, LANES))
    ones_hi = jnp.broadcast_to((1.0 - lo_f).astype(BF16), (3 * BLK, LANES))
    sink_pairs = [jnp.where(lo_lane, sink_ref[g], sink_ref[N_PAIRS + g]) * LOG2E
                  for g in range(N_PAIRS)]

    def qk(i):
        rows = slice(i * BLK, i * BLK + 3 * BLK)
        k_cat = jnp.concatenate([kv_s[rows, 0:LANES], kv_s[rows, LANES:2 * LANES]], axis=0)
        qb = q_s[i * BLK:(i + 1) * BLK, :]
        q_stack = jnp.concatenate([qb[:, g * LANES:(g + 1) * LANES] for g in range(N_PAIRS)], axis=0)
        return _dot_nt(q_stack, k_cat)

    def softmax(i, s_all):
        if i == 0:
            variant = jnp.where(t_idx == 0, 1, 0)
        elif i == nblk - 1:
            variant = jnp.where(t_idx == n_t - 1, 2, 0)
        else:
            variant = 0
        keep = keep_ref[variant]
        p_rows, m_rows = [], []
        for g in range(N_PAIRS):
            p_halves, m_halves = [], []
            for half in range(2):
                head = g + N_PAIRS * half
                s = s_all[g * BLK:(g + 1) * BLK, half * 3 * BLK:(half + 1) * 3 * BLK]
                sc = s * keep + bias_ref[variant * N_HEADS + head]
                m = jnp.maximum(jnp.max(sc, axis=-1, keepdims=True), sink_ref[head] * LOG2E)
                p_halves.append(jnp.exp2(sc - m).astype(BF16))
                m_halves.append(m)
            p_rows.append(jnp.concatenate(p_halves, axis=1))
            m_rows.append(jnp.where(lo_lane, m_halves[0], m_halves[1]))
        return jnp.concatenate(p_rows, axis=0), m_rows

    def pv(i, p_all):
        rows = slice(i * BLK, i * BLK + 3 * BLK)
        v_ext = jnp.concatenate([
            jnp.concatenate([kv_s[rows, 2 * LANES:3 * LANES], ones_lo], axis=1),
            jnp.concatenate([kv_s[rows, 3 * LANES:4 * LANES], ones_hi], axis=1)], axis=0)
        return _dot(p_all, v_ext)

    def finish_attn(i, o_ext, m_rows):
        pairs = []
        for g in range(N_PAIRS):
            og = o_ext[g * BLK:(g + 1) * BLK]
            den = og[:, LANES:2 * LANES] + jnp.exp2(sink_pairs[g] - m_rows[g])
            pairs.append(og[:, 0:LANES] / den)
        attn = jnp.concatenate(pairs, axis=1)
        attn = attn * _rms_scale(attn) * g_attn_ref[...]
        at_s[i * BLK:(i + 1) * BLK, :] = attn.astype(BF16)

    def out_proj_attn(r0, rows):
        o_ref[0, r0:r0 + rows, :] += _dot(at_s[r0:r0 + rows, :], w_out_ref[0:ATTN_WIDTH, :])

    lookahead = 2
    scores = {i: qk(i) for i in range(min(lookahead, nblk))}
    o_ref[0] = x_ref[0] + _dot(gmlp_out(), w_out_ref[ATTN_WIDTH:ATTN_WIDTH + SGU_WIDTH, :])
    half = nblk // 2
    for i in range(nblk):
        p_all, m_rows = softmax(i, scores.pop(i))
        o_ext = pv(i, p_all)
        finish_attn(i, o_ext, m_rows)
        if i + lookahead < nblk:
            scores[i + lookahead] = qk(i + lookahead)
        if i == nblk - 2:
            out_proj_attn(0, half * BLK)
    out_proj_attn(half * BLK, ts - half * BLK)


def _ffn_kernel(h_ref, hp_ref, hn_ref, p_ref, g_ffn_ref, w_up_ref, conv_w_ref, conv_b_ref,
                w_down_ref, g_ple_ref, w_gate_ref, w_proj_ref, g_final_ref, o_ref,
                act_s, *, final_norm):
    ts = h_ref.shape[1]
    t_idx = pl.program_id(1)
    n_t = pl.num_programs(1)

    h = h_ref[0]
    c = (h * _rms_scale(h) * g_ffn_ref[...]).astype(BF16)
    hh = jnp.concatenate([hp_ref[0], hn_ref[0]], axis=0)
    ch = (hh * _rms_scale(hh) * g_ffn_ref[...]).astype(BF16)
    c_ext = jnp.concatenate([c, ch], axis=0)
    row = lax.broadcasted_iota(jnp.int32, (2 * SUBLANES, 1), 0)
    has_prev = (t_idx > 0).astype(F32)
    has_next = (t_idx < n_t - 1).astype(F32)
    edge_keep = jnp.where(row < SUBLANES, has_prev, has_next) > 0.5
    row8 = lax.broadcasted_iota(jnp.int32, (SUBLANES, 1), 0)

    def up_proj(j):
        c0 = j * FF_CHUNK
        gate_all = _dot(c_ext, w_up_ref[:, c0:c0 + FF_CHUNK])
        val = _dot(c, w_up_ref[:, D_FF + c0:D_FF + c0 + FF_CHUNK])
        gate_h = jnp.where(edge_keep, gate_all[ts:ts + 2 * SUBLANES], 0.0)
        return gate_all[0:ts], val, gate_h

    n_chunks = D_FF // FF_CHUNK
    for j in range(n_chunks):
        c0 = j * FF_CHUNK
        gate, val, gate_h = up_proj(j)
        g_prev = pltpu.roll(gate, 1, axis=0)
        g_next = pltpu.roll(gate, ts - 1, axis=0)
        first = jnp.where(row8 == 0, gate_h[SUBLANES - 1:SUBLANES], g_prev[0:SUBLANES])
        last = jnp.where(row8 == SUBLANES - 1, gate_h[SUBLANES:SUBLANES + 1], g_next[ts - SUBLANES:ts])
        g_prev = jnp.concatenate([first, g_prev[SUBLANES:]], axis=0)
        g_next = jnp.concatenate([g_next[:ts - SUBLANES], last], axis=0)
        cw = conv_w_ref[:, c0:c0 + FF_CHUNK]
        conv = (g_prev * cw[0:1] + gate * cw[1:2] + g_next * cw[2:3]
                + conv_b_ref[:, c0:c0 + FF_CHUNK])
        act_s[:, c0:c0 + FF_CHUNK] = (jax.nn.gelu(conv, approximate=True) * val).astype(BF16)

    h2 = h_ref[0] + _dot(act_s[...], w_down_ref[...])
    cg = (h2 * _rms_scale(h2) * g_ple_ref[...]).astype(BF16)
    pg = jax.nn.sigmoid(_dot(cg, w_gate_ref[...]))
    h3 = h2 + _dot(p_ref[0].astype(BF16), w_proj_ref[...]) * pg
    if final_norm:
        h3 = h3 * _rms_scale(h3) * g_final_ref[...]
    o_ref[0] = h3


def _const_spec(shape):
    nd = len(shape)
    return pl.BlockSpec(shape, lambda b, t: (0,) * nd, pipeline_mode=pl.Buffered(1))


def _q_perm():
    perm = np.empty((ATTN_WIDTH,), np.int32)
    for n in range(ATTN_WIDTH):
        g, r = divmod(n, LANES)
        head = g if r < HEAD_DIM else N_PAIRS + g
        perm[n] = head * HEAD_DIM + r % HEAD_DIM
    return perm


def _mask_tables():
    qi = np.arange(BLK)[:, None]
    kj = np.arange(3 * BLK)[None, :]
    dist = np.abs(qi + BLK - kj)
    in_window = dist <= WINDOW
    key_ok = [np.ones_like(in_window), kj >= BLK, kj < 2 * BLK]
    bias = np.empty((N_MASK_VARIANTS, N_HEADS, BLK, 3 * BLK), np.float32)
    keep = np.empty((N_MASK_VARIANTS, BLK, 3 * BLK), np.float32)
    for v in range(N_MASK_VARIANTS):
        valid = in_window & key_ok[v]
        keep[v] = valid
        for h in range(N_HEADS):
            bias[v, h] = np.where(valid, -ALIBI_SLOPES[h] * dist * LOG2E, NEG_INF)
    return bias.reshape(N_MASK_VARIANTS * N_HEADS, BLK, 3 * BLK), keep


def kernel(x, p, g_mix, w_in, attn_sink, sgu_ln_g, sgu_ln_b, sgu_w, sgu_b, g_attn_out, g_sgu_out,
           w_out, g_ffn, w_up, conv_w, conv_b, w_down, g_ple, w_ple_gate, w_ple_proj, g_final):
    depth = w_in.shape[0]
    bsz, seq, d = x.shape
    assert d == D_MODEL and seq % MIX_TILE == 0 and seq % FFN_TILE == 0
    assert MIX_TILE // BLK >= 2 and seq // BLK >= 2
    perm = _q_perm()
    bias_tbl, keep_tbl = _mask_tables()
    row = lambda v: v.reshape(1, -1).astype(F32)

    h = x
    for i in range(depth):
        zu0 = ATTN_WIDTH + 2 * KV_WIDTH
        w_in_i = jnp.concatenate([w_in[i][:, zu0 + SGU_WIDTH:], w_in[i][:, zu0:zu0 + SGU_WIDTH],
                                  w_in[i][:, perm], w_in[i][:, ATTN_WIDTH:zu0]], axis=1).astype(BF16)
        w_out_i = jnp.concatenate([w_out[i][perm], w_out[i][ATTN_WIDTH:]], axis=0).astype(BF16)
        g_attn_i = row(g_attn_out[i][perm])
        wcat = sgu_w[i].reshape(SGU_GROUPS // 2, 2, BLK, BLK).transpose(0, 2, 1, 3)
        wcat = wcat.reshape(SGU_GROUPS // 2, BLK, 2 * BLK).astype(BF16)
        bst = jnp.repeat(sgu_b[i].T, SGU_WIDTH // SGU_GROUPS, axis=1).astype(F32)

        nb = seq // BLK
        mix_steps = seq // MIX_TILE
        bpt = MIX_TILE // BLK
        mix = pl.pallas_call(
            _mix_kernel,
            out_shape=jax.ShapeDtypeStruct((bsz, seq, D_MODEL), F32),
            grid=(bsz, mix_steps),
            in_specs=[
                pl.BlockSpec(memory_space=pltpu.SMEM),
                pl.BlockSpec((1, MIX_TILE, D_MODEL), lambda b, t: (b, t, 0)),
                pl.BlockSpec((1, BLK, D_MODEL), lambda b, t: (b, jnp.maximum(t * bpt - 1, 0), 0)),
                pl.BlockSpec((1, BLK, D_MODEL), lambda b, t: (b, jnp.minimum((t + 1) * bpt, nb - 1), 0)),
                _const_spec((1, D_MODEL)),
                _const_spec((D_MODEL, IN_WIDTH)),
                _const_spec((1, SGU_WIDTH)),
                _const_spec((1, SGU_WIDTH)),
                _const_spec((SGU_GROUPS // 2, BLK, 2 * BLK)),
                _const_spec((BLK, SGU_WIDTH)),
                _const_spec((1, ATTN_WIDTH)),
                _const_spec((1, SGU_WIDTH)),
                _const_spec((D_MODEL, D_MODEL)),
                _const_spec((N_MASK_VARIANTS * N_HEADS, BLK, 3 * BLK)),
                _const_spec((N_MASK_VARIANTS, BLK, 3 * BLK)),
            ],
            out_specs=pl.BlockSpec((1, MIX_TILE, D_MODEL), lambda b, t: (b, t, 0)),
            scratch_shapes=[
                pltpu.VMEM((MIX_TILE, ATTN_WIDTH), BF16),
                pltpu.VMEM((MIX_TILE + 2 * BLK, 4 * LANES), BF16),
                pltpu.VMEM((MIX_TILE, ATTN_WIDTH), BF16),
            ],
            compiler_params=pltpu.CompilerParams(
                dimension_semantics=("parallel", "parallel"),
                vmem_limit_bytes=VMEM_LIMIT_BYTES),
            name="mix_kernel",
        )
        h1 = mix(attn_sink[i].astype(F32), h, h, h, row(g_mix[i]), w_in_i, row(sgu_ln_g[i]),
                 row(sgu_ln_b[i]), wcat, bst, g_attn_i, row(g_sgu_out[i]), w_out_i,
                 bias_tbl, keep_tbl)

        ffn_steps = seq // FFN_TILE
        spt = FFN_TILE // SUBLANES
        ns = seq // SUBLANES
        ffn = pl.pallas_call(
            functools.partial(_ffn_kernel, final_norm=(i == depth - 1)),
            out_shape=jax.ShapeDtypeStruct((bsz, seq, D_MODEL), F32),
            grid=(bsz, ffn_steps),
            in_specs=[
                pl.BlockSpec((1, FFN_TILE, D_MODEL), lambda b, t: (b, t, 0)),
                pl.BlockSpec((1, SUBLANES, D_MODEL), lambda b, t: (b, jnp.maximum(t * spt - 1, 0), 0)),
                pl.BlockSpec((1, SUBLANES, D_MODEL), lambda b, t: (b, jnp.minimum((t + 1) * spt, ns - 1), 0)),
                pl.BlockSpec((1, FFN_TILE, PLE_DIM), lambda b, t: (b, t, 0)),
                _const_spec((1, D_MODEL)),
                _const_spec((D_MODEL, 2 * D_FF)),
                _const_spec((3, D_FF)),
                _const_spec((1, D_FF)),
                _const_spec((D_FF, D_MODEL)),
                _const_spec((1, D_MODEL)),
                _const_spec((D_MODEL, D_MODEL)),
                _const_spec((PLE_DIM, D_MODEL)),
                _const_spec((1, D_MODEL)),
            ],
            out_specs=pl.BlockSpec((1, FFN_TILE, D_MODEL), lambda b, t: (b, t, 0)),
            scratch_shapes=[
                pltpu.VMEM((FFN_TILE, D_FF), BF16),
            ],
            compiler_params=pltpu.CompilerParams(
                dimension_semantics=("parallel", "parallel"),
                vmem_limit_bytes=VMEM_LIMIT_BYTES),
            name="ffn_kernel",
        )
        h = ffn(h1, h1, h1, p[i], row(g_ffn[i]), w_up[i].astype(BF16), conv_w[i].astype(F32),
                row(conv_b[i]), w_down[i].astype(BF16), row(g_ple[i]), w_ple_gate[i].astype(BF16),
                w_ple_proj[i].astype(BF16), row(g_final))
    return h
```

```python
import functools
import math

import jax
import jax.numpy as jnp
import numpy as np
from jax import lax
from jax.experimental import pallas as pl
from jax.experimental.pallas import tpu as pltpu

D_MODEL = 1024
ATTN_WIDTH = 512
SGU_WIDTH = 512
HEAD_DIM = 64
N_HEADS = 8
N_KV = 2
GQ = 4
KV_WIDTH = N_KV * HEAD_DIM
WINDOW = 128
BLK = 128
SGU_GROUPS = 8
D_FF = 2816
PLE_DIM = 256
IN_WIDTH = ATTN_WIDTH + 2 * KV_WIDTH + 2 * SGU_WIDTH
RMS_EPS = 1e-6
LN_EPS = 1e-5
NEG_INF = -1e30

LANES = 128
SUBLANES = 8
N_PAIRS = N_HEADS // 2
ALIBI_SLOPES = tuple(2.0 ** (-8.0 * (h + 1) / N_HEADS) for h in range(N_HEADS))
LOG2E = math.log2(math.e)
Q_SCALE = HEAD_DIM ** -0.5 * LOG2E
N_MASK_VARIANTS = 3

MIX_TILE = 1024
FFN_TILE = 512
FF_CHUNK = 256
VMEM_LIMIT_BYTES = 56 * 1024 * 1024

BF16 = jnp.bfloat16
F32 = jnp.float32


def _rms_scale(x):
    return lax.rsqrt(jnp.mean(x * x, axis=-1, keepdims=True) + RMS_EPS)


def _dot(a, b):
    return jnp.dot(a, b, preferred_element_type=F32)


def _dot_nt(a, b):
    return lax.dot_general(a, b, (((1,), (1,)), ((), ())), preferred_element_type=F32)


def _mix_kernel(sink_ref, x_ref, xp_ref, xn_ref, g_mix_ref, w_in_ref, ln_g_ref, ln_b_ref,
                wcat_ref, bst_ref, g_attn_ref, g_sgu_ref, w_out_ref, bias_ref, keep_ref, o_ref,
                q_s, kv_s, at_s):
    ts = x_ref.shape[1]
    nblk = ts // BLK
    t_idx = pl.program_id(1)
    n_t = pl.num_programs(1)
    lane = lax.broadcasted_iota(jnp.int32, (1, LANES), 1)
    lo_lane = lane < HEAD_DIM

    def split_lanes(v):
        zero = jnp.zeros_like(v)
        return jnp.where(lo_lane, v, zero), jnp.where(lo_lane, zero, v)

    kv_lo, kv_hi = 2 * SGU_WIDTH + ATTN_WIDTH, IN_WIDTH

    def store_kv(r0, rows, kv):
        k0, k1 = split_lanes(kv[:, 0:KV_WIDTH])
        v0, v1 = split_lanes(kv[:, KV_WIDTH:2 * KV_WIDTH])
        kv_s[r0:r0 + rows, :] = jnp.concatenate([k0, k1, v0, v1], axis=1)

    xh = jnp.concatenate([xp_ref[0], xn_ref[0]], axis=0)
    ah = (xh * _rms_scale(xh) * g_mix_ref[...]).astype(BF16)
    kvh = _dot(ah, w_in_ref[:, kv_lo:kv_hi]).astype(BF16)
    store_kv(0, BLK, kvh[0:BLK])
    store_kv(BLK + ts, BLK, kvh[BLK:2 * BLK])

    x = x_ref[0]
    a = (x * _rms_scale(x) * g_mix_ref[...]).astype(BF16)
    z_sgu = _dot(a, w_in_ref[:, 0:2 * SGU_WIDTH])
    zv = z_sgu[:, 0:SGU_WIDTH]
    zu = z_sgu[:, SGU_WIDTH:2 * SGU_WIDTH]

    z_att = _dot(a, w_in_ref[:, 2 * SGU_WIDTH:IN_WIDTH])
    q_s[...] = (z_att[:, 0:ATTN_WIDTH] * Q_SCALE).astype(BF16)
    store_kv(BLK, ts, z_att[:, ATTN_WIDTH:ATTN_WIDTH + 2 * KV_WIDTH].astype(BF16))

    gv = jax.nn.gelu(zv, approximate=True)
    mu = jnp.mean(gv, axis=-1, keepdims=True)
    var = jnp.mean(jnp.square(gv - mu), axis=-1, keepdims=True)
    vv = ((gv - mu) * lax.rsqrt(var + LN_EPS) * ln_g_ref[...] + ln_b_ref[...]).astype(BF16)
    mixed_pairs = []
    for j in range(SGU_GROUPS // 2):
        rhs = []
        for i in range(nblk):
            lo, hi = split_lanes(vv[i * BLK:(i + 1) * BLK, j * LANES:(j + 1) * LANES])
            rhs.append(jnp.concatenate([lo, hi], axis=0))
        mixed_pairs.append(_dot(wcat_ref[j], jnp.concatenate(rhs, axis=1)))

    def gmlp_out(i0, n):
        mixed_blocks = [
            jnp.concatenate([mp[:, i * LANES:(i + 1) * LANES] for mp in mixed_pairs], axis=1)
            + bst_ref[...] for i in range(i0, i0 + n)]
        sgu = (jax.nn.gelu(zu[i0 * BLK:(i0 + n) * BLK], approximate=True)
               * jnp.concatenate(mixed_blocks, axis=0))
        return (sgu * _rms_scale(sgu) * g_sgu_ref[...]).astype(BF16)

    lo_f = jnp.where(lo_lane, 1.0, 0.0)
    ones_lo = jnp.broadcast_to(lo_f.astype(BF16), (3 * BLK, LANES))
    ones_hi = jnp.broadcast_to((1.0 - lo_f).astype(BF16), (3 * BLK, LANES))
    sink_pairs = [jnp.where(lo_lane, sink_ref[g], sink_ref[N_PAIRS + g]) * LOG2E
                  for g in range(N_PAIRS)]

    def qk(i):
        rows = slice(i * BLK, i * BLK + 3 * BLK)
        k_cat = jnp.concatenate([kv_s[rows, 0:LANES], kv_s[rows, LANES:2 * LANES]], axis=0)
        qb = q_s[i * BLK:(i + 1) * BLK, :]
        q_stack = jnp.concatenate([qb[:, g * LANES:(g + 1) * LANES] for g in range(N_PAIRS)], axis=0)
        return _dot_nt(q_stack, k_cat)

    def softmax(i, s_all):
        if i == 0:
            variant = jnp.where(t_idx == 0, 1, 0)
        elif i == nblk - 1:
            variant = jnp.where(t_idx == n_t - 1, 2, 0)
        else:
            variant = 0
        keep_prev = keep_ref[variant, :, 0:BLK]
        keep_next = keep_ref[variant, :, 2 * BLK:3 * BLK]
        p_rows, m_rows = [], []
        for g in range(N_PAIRS):
            p_halves, m_halves = [], []
            for half in range(2):
                head = g + N_PAIRS * half
                s = s_all[g * BLK:(g + 1) * BLK, half * 3 * BLK:(half + 1) * 3 * BLK]
                kept = jnp.concatenate([s[:, 0:BLK] * keep_prev, s[:, BLK:2 * BLK],
                                        s[:, 2 * BLK:3 * BLK] * keep_next], axis=1)
                sc = kept + bias_ref[variant * N_HEADS + head]
                m = jnp.maximum(jnp.max(sc, axis=-1, keepdims=True), sink_ref[head] * LOG2E)
                p_halves.append(jnp.exp2(sc - m).astype(BF16))
                m_halves.append(m)
            p_rows.append(jnp.concatenate(p_halves, axis=1))
            m_rows.append(jnp.where(lo_lane, m_halves[0], m_halves[1]))
        return jnp.concatenate(p_rows, axis=0), m_rows

    def pv(i, p_all):
        rows = slice(i * BLK, i * BLK + 3 * BLK)
        v_ext = jnp.concatenate([
            jnp.concatenate([kv_s[rows, 2 * LANES:3 * LANES], ones_lo], axis=1),
            jnp.concatenate([kv_s[rows, 3 * LANES:4 * LANES], ones_hi], axis=1)], axis=0)
        return _dot(p_all, v_ext)

    def finish_attn(i, o_ext, m_rows):
        pairs = []
        for g in range(N_PAIRS):
            og = o_ext[g * BLK:(g + 1) * BLK]
            den = og[:, LANES:2 * LANES] + jnp.exp2(sink_pairs[g] - m_rows[g])
            pairs.append(og[:, 0:LANES] / den)
        attn = jnp.concatenate(pairs, axis=1)
        attn = attn * _rms_scale(attn) * g_attn_ref[...]
        at_s[i * BLK:(i + 1) * BLK, :] = attn.astype(BF16)

    half_cols = D_MODEL // 2

    def out_piece(kind, i0, c0):
        rows = slice(i0 * BLK, (i0 + 2) * BLK)
        cols = slice(c0, c0 + half_cols)
        if kind == "gmlp":
            o_ref[0, rows, cols] = x_ref[0, rows, cols] + _dot(
                sgu_rows[i0], w_out_ref[ATTN_WIDTH:ATTN_WIDTH + SGU_WIDTH, cols])
        else:
            o_ref[0, rows, cols] += _dot(at_s[rows, :], w_out_ref[0:ATTN_WIDTH, cols])

    sgu_rows = {}
    pieces = []
    for i0 in range(0, nblk, 2):
        sgu_rows[i0] = gmlp_out(i0, 2)
        pieces += [("gmlp", i0, 0), ("gmlp", i0, half_cols)]

    s_next = qk(0)
    for i in range(nblk):
        for _ in range(2 if len(pieces) >= 3 else min(1, len(pieces))):
            out_piece(*pieces.pop(0))
        p_all, m_rows = softmax(i, s_next)
        o_ext = pv(i, p_all)
        finish_attn(i, o_ext, m_rows)
        if i + 1 < nblk:
            s_next = qk(i + 1)
        if i % 2 == 1:
            pieces += [("attn", i - 1, 0), ("attn", i - 1, half_cols)]
    for piece in pieces:
        out_piece(*piece)


def _ffn_kernel(h_ref, hp_ref, hn_ref, p_ref, g_ffn_ref, w_up_ref, conv_w_ref, conv_b_ref,
                w_down_ref, g_ple_ref, w_gate_ref, w_proj_ref, g_final_ref, o_ref,
                act_s, *, final_norm):
    ts = h_ref.shape[1]
    t_idx = pl.program_id(1)
    n_t = pl.num_programs(1)

    ple = _dot(p_ref[0].astype(BF16), w_proj_ref[...])

    h = h_ref[0]
    c = (h * _rms_scale(h) * g_ffn_ref[...]).astype(BF16)
    hh = jnp.concatenate([hp_ref[0], hn_ref[0]], axis=0)
    ch = (hh * _rms_scale(hh) * g_ffn_ref[...]).astype(BF16)
    c_ext = jnp.concatenate([c, ch], axis=0)
    row = lax.broadcasted_iota(jnp.int32, (2 * SUBLANES, 1), 0)
    has_prev = (t_idx > 0).astype(F32)
    has_next = (t_idx < n_t - 1).astype(F32)
    edge_keep = jnp.where(row < SUBLANES, has_prev, has_next) > 0.5
    row8 = lax.broadcasted_iota(jnp.int32, (SUBLANES, 1), 0)

    def up_proj(j):
        c0 = j * FF_CHUNK
        gate_all = _dot(c_ext, w_up_ref[:, c0:c0 + FF_CHUNK])
        val = _dot(c, w_up_ref[:, D_FF + c0:D_FF + c0 + FF_CHUNK])
        gate_h = jnp.where(edge_keep, gate_all[ts:ts + 2 * SUBLANES], 0.0)
        return gate_all[0:ts], val, gate_h

    n_chunks = D_FF // FF_CHUNK
    for j in range(n_chunks):
        c0 = j * FF_CHUNK
        gate, val, gate_h = up_proj(j)
        g_prev = pltpu.roll(gate, 1, axis=0)
        g_next = pltpu.roll(gate, ts - 1, axis=0)
        first = jnp.where(row8 == 0, gate_h[SUBLANES - 1:SUBLANES], g_prev[0:SUBLANES])
        last = jnp.where(row8 == SUBLANES - 1, gate_h[SUBLANES:SUBLANES + 1], g_next[ts - SUBLANES:ts])
        g_prev = jnp.concatenate([first, g_prev[SUBLANES:]], axis=0)
        g_next = jnp.concatenate([g_next[:ts - SUBLANES], last], axis=0)
        cw = conv_w_ref[:, c0:c0 + FF_CHUNK]
        conv = (g_prev * cw[0:1] + gate * cw[1:2] + g_next * cw[2:3]
                + conv_b_ref[:, c0:c0 + FF_CHUNK])
        act_s[:, c0:c0 + FF_CHUNK] = (jax.nn.gelu(conv, approximate=True) * val).astype(BF16)

    n_split = 2
    rows = [slice(k * ts // n_split, (k + 1) * ts // n_split) for k in range(n_split)]
    h2 = [h_ref[0, r, :] + _dot(act_s[r, :], w_down_ref[...]) for r in rows]
    gates = [_dot((h2k * _rms_scale(h2k) * g_ple_ref[...]).astype(BF16), w_gate_ref[...]) for h2k in h2]
    for r, h2k, gate in zip(rows, h2, gates):
        h3 = h2k + ple[r, :] * (0.5 * (jnp.tanh(0.5 * gate) + 1.0))
        if final_norm:
            h3 = h3 * _rms_scale(h3) * g_final_ref[...]
        o_ref[0, r, :] = h3


def _const_spec(shape):
    nd = len(shape)
    return pl.BlockSpec(shape, lambda b, t: (0,) * nd, pipeline_mode=pl.Buffered(1))


def _pair_heads(t, axis):
    shape = t.shape
    t = t.reshape(shape[:axis] + (N_KV, N_PAIRS, HEAD_DIM) + shape[axis + 1:])
    t = jnp.swapaxes(t, axis, axis + 1)
    return t.reshape(shape)


def _mask_tables():
    qi = np.arange(BLK)[:, None]
    kj = np.arange(3 * BLK)[None, :]
    dist = np.abs(qi + BLK - kj)
    in_window = dist <= WINDOW
    key_ok = [np.ones_like(in_window), kj >= BLK, kj < 2 * BLK]
    bias = np.empty((N_MASK_VARIANTS, N_HEADS, BLK, 3 * BLK), np.float32)
    keep = np.empty((N_MASK_VARIANTS, BLK, 3 * BLK), np.float32)
    for v in range(N_MASK_VARIANTS):
        valid = in_window & key_ok[v]
        keep[v] = valid
        for h in range(N_HEADS):
            bias[v, h] = np.where(valid, -ALIBI_SLOPES[h] * dist * LOG2E, NEG_INF)
    return bias.reshape(N_MASK_VARIANTS * N_HEADS, BLK, 3 * BLK), keep


def kernel(x, p, g_mix, w_in, attn_sink, sgu_ln_g, sgu_ln_b, sgu_w, sgu_b, g_attn_out, g_sgu_out,
           w_out, g_ffn, w_up, conv_w, conv_b, w_down, g_ple, w_ple_gate, w_ple_proj, g_final):
    depth = w_in.shape[0]
    bsz, seq, d = x.shape
    assert d == D_MODEL and seq % MIX_TILE == 0 and seq % FFN_TILE == 0
    assert (MIX_TILE // BLK) % 2 == 0 and seq // BLK >= 2
    bias_tbl, keep_tbl = _mask_tables()
    row = lambda v: v.reshape(1, -1).astype(F32)

    h = x
    for i in range(depth):
        zu0 = ATTN_WIDTH + 2 * KV_WIDTH
        w_in_b = w_in[i].astype(BF16)
        w_out_b = w_out[i].astype(BF16)
        w_in_i = jnp.concatenate([w_in_b[:, zu0 + SGU_WIDTH:], w_in_b[:, zu0:zu0 + SGU_WIDTH],
                                  _pair_heads(w_in_b[:, :ATTN_WIDTH], 1), w_in_b[:, ATTN_WIDTH:zu0]],
                                 axis=1)
        w_out_i = jnp.concatenate([_pair_heads(w_out_b[:ATTN_WIDTH], 0), w_out_b[ATTN_WIDTH:]], axis=0)
        g_attn_i = row(_pair_heads(g_attn_out[i], 0))
        wcat = sgu_w[i].reshape(SGU_GROUPS // 2, 2, BLK, BLK).transpose(0, 2, 1, 3)
        wcat = wcat.reshape(SGU_GROUPS // 2, BLK, 2 * BLK).astype(BF16)
        bst = jnp.repeat(sgu_b[i].T, SGU_WIDTH // SGU_GROUPS, axis=1).astype(F32)

        nb = seq // BLK
        mix_steps = seq // MIX_TILE
        bpt = MIX_TILE // BLK
        mix = pl.pallas_call(
            _mix_kernel,
            out_shape=jax.ShapeDtypeStruct((bsz, seq, D_MODEL), F32),
            grid=(bsz, mix_steps),
            in_specs=[
                pl.BlockSpec(memory_space=pltpu.SMEM),
                pl.BlockSpec((1, MIX_TILE, D_MODEL), lambda b, t: (b, t, 0)),
                pl.BlockSpec((1, BLK, D_MODEL), lambda b, t: (b, jnp.maximum(t * bpt - 1, 0), 0)),
                pl.BlockSpec((1, BLK, D_MODEL), lambda b, t: (b, jnp.minimum((t + 1) * bpt, nb - 1), 0)),
                _const_spec((1, D_MODEL)),
                _const_spec((D_MODEL, IN_WIDTH)),
                _const_spec((1, SGU_WIDTH)),
                _const_spec((1, SGU_WIDTH)),
                _const_spec((SGU_GROUPS // 2, BLK, 2 * BLK)),
                _const_spec((BLK, SGU_WIDTH)),
                _const_spec((1, ATTN_WIDTH)),
                _const_spec((1, SGU_WIDTH)),
                _const_spec((D_MODEL, D_MODEL)),
                _const_spec((N_MASK_VARIANTS * N_HEADS, BLK, 3 * BLK)),
                _const_spec((N_MASK_VARIANTS, BLK, 3 * BLK)),
            ],
            out_specs=pl.BlockSpec((1, MIX_TILE, D_MODEL), lambda b, t: (b, t, 0)),
            scratch_shapes=[
                pltpu.VMEM((MIX_TILE, ATTN_WIDTH), BF16),
                pltpu.VMEM((MIX_TILE + 2 * BLK, 4 * LANES), BF16),
                pltpu.VMEM((MIX_TILE, ATTN_WIDTH), BF16),
            ],
            compiler_params=pltpu.CompilerParams(
                dimension_semantics=("parallel", "parallel"),
                vmem_limit_bytes=VMEM_LIMIT_BYTES),
            name="mix_kernel",
        )
        h1 = mix(attn_sink[i].astype(F32), h, h, h, row(g_mix[i]), w_in_i, row(sgu_ln_g[i]),
                 row(sgu_ln_b[i]), wcat, bst, g_attn_i, row(g_sgu_out[i]), w_out_i,
                 bias_tbl, keep_tbl)

        ffn_steps = seq // FFN_TILE
        spt = FFN_TILE // SUBLANES
        ns = seq // SUBLANES
        ffn = pl.pallas_call(
            functools.partial(_ffn_kernel, final_norm=(i == depth - 1)),
            out_shape=jax.ShapeDtypeStruct((bsz, seq, D_MODEL), F32),
            grid=(bsz, ffn_steps),
            in_specs=[
                pl.BlockSpec((1, FFN_TILE, D_MODEL), lambda b, t: (b, t, 0)),
                pl.BlockSpec((1, SUBLANES, D_MODEL), lambda b, t: (b, jnp.maximum(t * spt - 1, 0), 0)),
                pl.BlockSpec((1, SUBLANES, D_MODEL), lambda b, t: (b, jnp.minimum((t + 1) * spt, ns - 1), 0)),
                pl.BlockSpec((1, FFN_TILE, PLE_DIM), lambda b, t: (b, t, 0)),
                _const_spec((1, D_MODEL)),
                _const_spec((D_MODEL, 2 * D_FF)),
                _const_spec((3, D_FF)),
                _const_spec((1, D_FF)),
                _const_spec((D_FF, D_MODEL)),
                _const_spec((1, D_MODEL)),
                _const_spec((D_MODEL, D_MODEL)),
                _const_spec((PLE_DIM, D_MODEL)),
                _const_spec((1, D_MODEL)),
            ],
            out_specs=pl.BlockSpec((1, FFN_TILE, D_MODEL), lambda b, t: (b, t, 0)),
            scratch_shapes=[
                pltpu.VMEM((FFN_TILE, D_FF), BF16),
            ],
            compiler_params=pltpu.CompilerParams(
                dimension_semantics=("parallel", "parallel"),
                vmem_limit_bytes=VMEM_LIMIT_BYTES),
            name="ffn_kernel",
        )
        h = ffn(h1, h1, h1, p[i], row(g_ffn[i]), w_up[i].astype(BF16), conv_w[i].astype(F32),
                row(conv_b[i]), w_down[i].astype(BF16), row(g_ple[i]), w_ple_gate[i].astype(BF16),
                w_ple_proj[i].astype(BF16), row(g_final))
    return h
```

```python
import functools
import math

import jax
import jax.numpy as jnp
import numpy as np
from jax import lax
from jax.experimental import pallas as pl
from jax.experimental.pallas import tpu as pltpu

D_MODEL = 1024
ATTN_WIDTH = 512
SGU_WIDTH = 512
HEAD_DIM = 64
N_HEADS = 8
N_KV = 2
GQ = 4
KV_WIDTH = N_KV * HEAD_DIM
WINDOW = 128
BLK = 128
SGU_GROUPS = 8
D_FF = 2816
PLE_DIM = 256
IN_WIDTH = ATTN_WIDTH + 2 * KV_WIDTH + 2 * SGU_WIDTH
RMS_EPS = 1e-6
LN_EPS = 1e-5
NEG_INF = -1e30

LANES = 128
SUBLANES = 8
N_PAIRS = N_HEADS // 2
ALIBI_SLOPES = tuple(2.0 ** (-8.0 * (h + 1) / N_HEADS) for h in range(N_HEADS))
LOG2E = math.log2(math.e)
Q_SCALE = HEAD_DIM ** -0.5 * LOG2E
N_MASK_VARIANTS = 3

MIX_TILE = 1024
FFN_TILE = 1024
FF_CHUNK = 256
VMEM_LIMIT_BYTES = 56 * 1024 * 1024

BF16 = jnp.bfloat16
F32 = jnp.float32


def _rms_scale(x):
    return lax.rsqrt(jnp.mean(x * x, axis=-1, keepdims=True) + RMS_EPS)


def _dot(a, b):
    return jnp.dot(a, b, preferred_element_type=F32)


def _dot_nt(a, b):
    return lax.dot_general(a, b, (((1,), (1,)), ((), ())), preferred_element_type=F32)


def _mix_kernel(sink_ref, x_ref, xp_ref, xn_ref, g_mix_ref, w_in_ref, ln_g_ref, ln_b_ref,
                wcat_ref, bst_ref, g_attn_ref, g_sgu_ref, w_out_ref, bias_ref, keep_ref, o_ref,
                q_s, kv_s, at_s):
    ts = x_ref.shape[1]
    nblk = ts // BLK
    t_idx = pl.program_id(1)
    n_t = pl.num_programs(1)
    lane = lax.broadcasted_iota(jnp.int32, (1, LANES), 1)
    lo_lane = lane < HEAD_DIM

    def split_lanes(v):
        zero = jnp.zeros_like(v)
        return jnp.where(lo_lane, v, zero), jnp.where(lo_lane, zero, v)

    kv_lo, kv_hi = 2 * SGU_WIDTH + ATTN_WIDTH, IN_WIDTH

    def store_kv(r0, rows, kv):
        k0, k1 = split_lanes(kv[:, 0:KV_WIDTH])
        v0, v1 = split_lanes(kv[:, KV_WIDTH:2 * KV_WIDTH])
        kv_s[r0:r0 + rows, :] = jnp.concatenate([k0, k1, v0, v1], axis=1)

    xh = jnp.concatenate([xp_ref[0], xn_ref[0]], axis=0)
    ah = (xh * _rms_scale(xh) * g_mix_ref[...]).astype(BF16)
    kvh = _dot(ah, w_in_ref[:, kv_lo:kv_hi]).astype(BF16)
    store_kv(0, BLK, kvh[0:BLK])
    store_kv(BLK + ts, BLK, kvh[BLK:2 * BLK])

    x = x_ref[0]
    a = (x * _rms_scale(x) * g_mix_ref[...]).astype(BF16)
    z_sgu = _dot(a, w_in_ref[:, 0:2 * SGU_WIDTH])
    zv = z_sgu[:, 0:SGU_WIDTH]
    zu = z_sgu[:, SGU_WIDTH:2 * SGU_WIDTH]

    z_att = _dot(a, w_in_ref[:, 2 * SGU_WIDTH:IN_WIDTH])
    q_s[...] = (z_att[:, 0:ATTN_WIDTH] * Q_SCALE).astype(BF16)
    store_kv(BLK, ts, z_att[:, ATTN_WIDTH:ATTN_WIDTH + 2 * KV_WIDTH].astype(BF16))

    gv = jax.nn.gelu(zv, approximate=True)
    mu = jnp.mean(gv, axis=-1, keepdims=True)
    var = jnp.mean(jnp.square(gv - mu), axis=-1, keepdims=True)
    vv = ((gv - mu) * lax.rsqrt(var + LN_EPS) * ln_g_ref[...] + ln_b_ref[...]).astype(BF16)
    mixed_pairs = []
    for j in range(SGU_GROUPS // 2):
        rhs = []
        for i in range(nblk):
            lo, hi = split_lanes(vv[i * BLK:(i + 1) * BLK, j * LANES:(j + 1) * LANES])
            rhs.append(jnp.concatenate([lo, hi], axis=0))
        mixed_pairs.append(_dot(wcat_ref[j], jnp.concatenate(rhs, axis=1)))

    def gmlp_out(i0, n):
        mixed_blocks = [
            jnp.concatenate([mp[:, i * LANES:(i + 1) * LANES] for mp in mixed_pairs], axis=1)
            + bst_ref[...] for i in range(i0, i0 + n)]
        sgu = (jax.nn.gelu(zu[i0 * BLK:(i0 + n) * BLK], approximate=True)
               * jnp.concatenate(mixed_blocks, axis=0))
        return (sgu * _rms_scale(sgu) * g_sgu_ref[...]).astype(BF16)

    lo_f = jnp.where(lo_lane, 1.0, 0.0)
    ones_lo = jnp.broadcast_to(lo_f.astype(BF16), (3 * BLK, LANES))
    ones_hi = jnp.broadcast_to((1.0 - lo_f).astype(BF16), (3 * BLK, LANES))
    sink_pairs = [jnp.where(lo_lane, sink_ref[g], sink_ref[N_PAIRS + g]) * LOG2E
                  for g in range(N_PAIRS)]

    def qk(i):
        rows = slice(i * BLK, i * BLK + 3 * BLK)
        k_cat = jnp.concatenate([kv_s[rows, 0:LANES], kv_s[rows, LANES:2 * LANES]], axis=0)
        qb = q_s[i * BLK:(i + 1) * BLK, :]
        q_stack = jnp.concatenate([qb[:, g * LANES:(g + 1) * LANES] for g in range(N_PAIRS)], axis=0)
        return _dot_nt(q_stack, k_cat)

    def softmax(i, s_all):
        if i == 0:
            variant = jnp.where(t_idx == 0, 1, 0)
        elif i == nblk - 1:
            variant = jnp.where(t_idx == n_t - 1, 2, 0)
        else:
            variant = 0
        keep_prev = keep_ref[variant, :, 0:BLK]
        keep_next = keep_ref[variant, :, 2 * BLK:3 * BLK]
        p_rows, m_rows = [], []
        for g in range(N_PAIRS):
            p_halves, m_halves = [], []
            for half in range(2):
                head = g + N_PAIRS * half
                s = s_all[g * BLK:(g + 1) * BLK, half * 3 * BLK:(half + 1) * 3 * BLK]
                kept = jnp.concatenate([s[:, 0:BLK] * keep_prev, s[:, BLK:2 * BLK],
                                        s[:, 2 * BLK:3 * BLK] * keep_next], axis=1)
                sc = kept + bias_ref[variant * N_HEADS + head]
                m = jnp.maximum(jnp.max(sc, axis=-1, keepdims=True), sink_ref[head] * LOG2E)
                p_halves.append(jnp.exp2(sc - m).astype(BF16))
                m_halves.append(m)
            p_rows.append(jnp.concatenate(p_halves, axis=1))
            m_rows.append(jnp.where(lo_lane, m_halves[0], m_halves[1]))
        return jnp.concatenate(p_rows, axis=0), m_rows

    def pv(i, p_all):
        rows = slice(i * BLK, i * BLK + 3 * BLK)
        v_ext = jnp.concatenate([
            jnp.concatenate([kv_s[rows, 2 * LANES:3 * LANES], ones_lo], axis=1),
            jnp.concatenate([kv_s[rows, 3 * LANES:4 * LANES], ones_hi], axis=1)], axis=0)
        return _dot(p_all, v_ext)

    def finish_attn(i, o_ext, m_rows):
        pairs = []
        for g in range(N_PAIRS):
            og = o_ext[g * BLK:(g + 1) * BLK]
            den = og[:, LANES:2 * LANES] + jnp.exp2(sink_pairs[g] - m_rows[g])
            pairs.append(og[:, 0:LANES] / den)
        attn = jnp.concatenate(pairs, axis=1)
        attn = attn * _rms_scale(attn) * g_attn_ref[...]
        at_s[i * BLK:(i + 1) * BLK, :] = attn.astype(BF16)

    half_cols = D_MODEL // 2

    def out_piece(kind, i0, c0):
        rows = slice(i0 * BLK, (i0 + 2) * BLK)
        cols = slice(c0, c0 + half_cols)
        if kind == "gmlp":
            o_ref[0, rows, cols] = x_ref[0, rows, cols] + _dot(
                sgu_rows[i0], w_out_ref[ATTN_WIDTH:ATTN_WIDTH + SGU_WIDTH, cols])
        else:
            o_ref[0, rows, cols] += _dot(at_s[rows, :], w_out_ref[0:ATTN_WIDTH, cols])

    sgu_rows = {}
    pieces = []
    for i0 in range(0, nblk, 2):
        sgu_rows[i0] = gmlp_out(i0, 2)
        pieces += [("gmlp", i0, 0), ("gmlp", i0, half_cols)]

    s_next = qk(0)
    for i in range(nblk):
        for _ in range(2 if len(pieces) >= 3 else min(1, len(pieces))):
            out_piece(*pieces.pop(0))
        p_all, m_rows = softmax(i, s_next)
        o_ext = pv(i, p_all)
        finish_attn(i, o_ext, m_rows)
        if i + 1 < nblk:
            s_next = qk(i + 1)
        if i % 2 == 1:
            pieces += [("attn", i - 1, 0), ("attn", i - 1, half_cols)]
    for piece in pieces:
        out_piece(*piece)


def _ffn_kernel(h_ref, hp_ref, hn_ref, p_ref, g_ffn_ref, w_up_ref, conv_w_ref, conv_b_ref,
                w_down_ref, g_ple_ref, w_gate_ref, w_proj_ref, g_final_ref, o_ref,
                act_s, *, final_norm):
    ts = h_ref.shape[1]
    t_idx = pl.program_id(1)
    n_t = pl.num_programs(1)

    ple = _dot(p_ref[0].astype(BF16), w_proj_ref[...])

    h = h_ref[0]
    c = (h * _rms_scale(h) * g_ffn_ref[...]).astype(BF16)
    hh = jnp.concatenate([hp_ref[0], hn_ref[0]], axis=0)
    ch = (hh * _rms_scale(hh) * g_ffn_ref[...]).astype(BF16)
    c_ext = jnp.concatenate([c, ch], axis=0)
    row = lax.broadcasted_iota(jnp.int32, (2 * SUBLANES, 1), 0)
    has_prev = (t_idx > 0).astype(F32)
    has_next = (t_idx < n_t - 1).astype(F32)
    edge_keep = jnp.where(row < SUBLANES, has_prev, has_next) > 0.5
    row8 = lax.broadcasted_iota(jnp.int32, (SUBLANES, 1), 0)

    def up_proj(j):
        c0 = j * FF_CHUNK
        gate_all = _dot(c_ext, w_up_ref[:, c0:c0 + FF_CHUNK])
        val = _dot(c, w_up_ref[:, D_FF + c0:D_FF + c0 + FF_CHUNK])
        gate_h = jnp.where(edge_keep, gate_all[ts:ts + 2 * SUBLANES], 0.0)
        return gate_all[0:ts], val, gate_h

    n_chunks = D_FF // FF_CHUNK
    for j in range(n_chunks):
        c0 = j * FF_CHUNK
        gate, val, gate_h = up_proj(j)
        g_prev = pltpu.roll(gate, 1, axis=0)
        g_next = pltpu.roll(gate, ts - 1, axis=0)
        first = jnp.where(row8 == 0, gate_h[SUBLANES - 1:SUBLANES], g_prev[0:SUBLANES])
        last = jnp.where(row8 == SUBLANES - 1, gate_h[SUBLANES:SUBLANES + 1], g_next[ts - SUBLANES:ts])
        g_prev = jnp.concatenate([first, g_prev[SUBLANES:]], axis=0)
        g_next = jnp.concatenate([g_next[:ts - SUBLANES], last], axis=0)
        cw = conv_w_ref[:, c0:c0 + FF_CHUNK]
        conv = (g_prev * cw[0:1] + gate * cw[1:2] + g_next * cw[2:3]
                + conv_b_ref[:, c0:c0 + FF_CHUNK])
        act_s[:, c0:c0 + FF_CHUNK] = (jax.nn.gelu(conv, approximate=True) * val).astype(BF16)

    n_split = 2
    rows = [slice(k * ts // n_split, (k + 1) * ts // n_split) for k in range(n_split)]
    h2 = [h_ref[0, r, :] + _dot(act_s[r, :], w_down_ref[...]) for r in rows]
    gates = [_dot((h2k * _rms_scale(h2k) * g_ple_ref[...]).astype(BF16), w_gate_ref[...]) for h2k in h2]
    for r, h2k, gate in zip(rows, h2, gates):
        h3 = h2k + ple[r, :] * (0.5 * (jnp.tanh(0.5 * gate) + 1.0))
        if final_norm:
            h3 = h3 * _rms_scale(h3) * g_final_ref[...]
        o_ref[0, r, :] = h3


def _const_spec(shape):
    nd = len(shape)
    return pl.BlockSpec(shape, lambda b, t: (0,) * nd, pipeline_mode=pl.Buffered(1))


def _pair_heads(t, axis):
    shape = t.shape
    t = t.reshape(shape[:axis] + (N_KV, N_PAIRS, HEAD_DIM) + shape[axis + 1:])
    t = jnp.swapaxes(t, axis, axis + 1)
    return t.reshape(shape)


def _mask_tables():
    qi = np.arange(BLK)[:, None]
    kj = np.arange(3 * BLK)[None, :]
    dist = np.abs(qi + BLK - kj)
    in_window = dist <= WINDOW
    key_ok = [np.ones_like(in_window), kj >= BLK, kj < 2 * BLK]
    bias = np.empty((N_MASK_VARIANTS, N_HEADS, BLK, 3 * BLK), np.float32)
    keep = np.empty((N_MASK_VARIANTS, BLK, 3 * BLK), np.float32)
    for v in range(N_MASK_VARIANTS):
        valid = in_window & key_ok[v]
        keep[v] = valid
        for h in range(N_HEADS):
            bias[v, h] = np.where(valid, -ALIBI_SLOPES[h] * dist * LOG2E, NEG_INF)
    return bias.reshape(N_MASK_VARIANTS * N_HEADS, BLK, 3 * BLK), keep


def kernel(x, p, g_mix, w_in, attn_sink, sgu_ln_g, sgu_ln_b, sgu_w, sgu_b, g_attn_out, g_sgu_out,
           w_out, g_ffn, w_up, conv_w, conv_b, w_down, g_ple, w_ple_gate, w_ple_proj, g_final):
    depth = w_in.shape[0]
    bsz, seq, d = x.shape
    assert d == D_MODEL and seq % MIX_TILE == 0 and seq % FFN_TILE == 0
    assert (MIX_TILE // BLK) % 2 == 0 and seq // BLK >= 2
    bias_tbl, keep_tbl = _mask_tables()
    row = lambda v: v.reshape(1, -1).astype(F32)

    h = x
    for i in range(depth):
        zu0 = ATTN_WIDTH + 2 * KV_WIDTH
        w_in_b = w_in[i].astype(BF16)
        w_out_b = w_out[i].astype(BF16)
        w_in_i = jnp.concatenate([w_in_b[:, zu0 + SGU_WIDTH:], w_in_b[:, zu0:zu0 + SGU_WIDTH],
                                  _pair_heads(w_in_b[:, :ATTN_WIDTH], 1), w_in_b[:, ATTN_WIDTH:zu0]],
                                 axis=1)
        w_out_i = jnp.concatenate([_pair_heads(w_out_b[:ATTN_WIDTH], 0), w_out_b[ATTN_WIDTH:]], axis=0)
        g_attn_i = row(_pair_heads(g_attn_out[i], 0))
        wcat = sgu_w[i].reshape(SGU_GROUPS // 2, 2, BLK, BLK).transpose(0, 2, 1, 3)
        wcat = wcat.reshape(SGU_GROUPS // 2, BLK, 2 * BLK).astype(BF16)
        bst = jnp.repeat(sgu_b[i].T, SGU_WIDTH // SGU_GROUPS, axis=1).astype(F32)

        nb = seq // BLK
        mix_steps = seq // MIX_TILE
        bpt = MIX_TILE // BLK
        mix = pl.pallas_call(
            _mix_kernel,
            out_shape=jax.ShapeDtypeStruct((bsz, seq, D_MODEL), F32),
            grid=(bsz, mix_steps),
            in_specs=[
                pl.BlockSpec(memory_space=pltpu.SMEM),
                pl.BlockSpec((1, MIX_TILE, D_MODEL), lambda b, t: (b, t, 0)),
                pl.BlockSpec((1, BLK, D_MODEL), lambda b, t: (b, jnp.maximum(t * bpt - 1, 0), 0)),
                pl.BlockSpec((1, BLK, D_MODEL), lambda b, t: (b, jnp.minimum((t + 1) * bpt, nb - 1), 0)),
                _const_spec((1, D_MODEL)),
                _const_spec((D_MODEL, IN_WIDTH)),
                _const_spec((1, SGU_WIDTH)),
                _const_spec((1, SGU_WIDTH)),
                _const_spec((SGU_GROUPS // 2, BLK, 2 * BLK)),
                _const_spec((BLK, SGU_WIDTH)),
                _const_spec((1, ATTN_WIDTH)),
                _const_spec((1, SGU_WIDTH)),
                _const_spec((D_MODEL, D_MODEL)),
                _const_spec((N_MASK_VARIANTS * N_HEADS, BLK, 3 * BLK)),
                _const_spec((N_MASK_VARIANTS, BLK, 3 * BLK)),
            ],
            out_specs=pl.BlockSpec((1, MIX_TILE, D_MODEL), lambda b, t: (b, t, 0)),
            scratch_shapes=[
                pltpu.VMEM((MIX_TILE, ATTN_WIDTH), BF16),
                pltpu.VMEM((MIX_TILE + 2 * BLK, 4 * LANES), BF16),
                pltpu.VMEM((MIX_TILE, ATTN_WIDTH), BF16),
            ],
            compiler_params=pltpu.CompilerParams(
                dimension_semantics=("parallel", "parallel"),
                vmem_limit_bytes=VMEM_LIMIT_BYTES),
            name="mix_kernel",
        )
        h1 = mix(attn_sink[i].astype(F32), h, h, h, row(g_mix[i]), w_in_i, row(sgu_ln_g[i]),
                 row(sgu_ln_b[i]), wcat, bst, g_attn_i, row(g_sgu_out[i]), w_out_i,
                 bias_tbl, keep_tbl)

        ffn_steps = seq // FFN_TILE
        spt = FFN_TILE // SUBLANES
        ns = seq // SUBLANES
        ffn = pl.pallas_call(
            functools.partial(_ffn_kernel, final_norm=(i == depth - 1)),
            out_shape=jax.ShapeDtypeStruct((bsz, seq, D_MODEL), F32),
            grid=(bsz, ffn_steps),
            in_specs=[
                pl.BlockSpec((1, FFN_TILE, D_MODEL), lambda b, t: (b, t, 0)),
                pl.BlockSpec((1, SUBLANES, D_MODEL), lambda b, t: (b, jnp.maximum(t * spt - 1, 0), 0)),
                pl.BlockSpec((1, SUBLANES, D_MODEL), lambda b, t: (b, jnp.minimum((t + 1) * spt, ns - 1), 0)),
                pl.BlockSpec((1, FFN_TILE, PLE_DIM), lambda b, t: (b, t, 0)),
                _const_spec((1, D_MODEL)),
                _const_spec((D_MODEL, 2 * D_FF)),
                _const_spec((3, D_FF)),
                _const_spec((1, D_FF)),
                _const_spec((D_FF, D_MODEL)),
                _const_spec((1, D_MODEL)),
                _const_spec((D_MODEL, D_MODEL)),
                _const_spec((PLE_DIM, D_MODEL)),
                _const_spec((1, D_MODEL)),
            ],
            out_specs=pl.BlockSpec((1, FFN_TILE, D_MODEL), lambda b, t: (b, t, 0)),
            scratch_shapes=[
                pltpu.VMEM((FFN_TILE, D_FF), BF16),
            ],
            compiler_params=pltpu.CompilerParams(
                dimension_semantics=("parallel", "parallel"),
                vmem_limit_bytes=VMEM_LIMIT_BYTES),
            name="ffn_kernel",
        )
        h = ffn(h1, h1, h1, p[i], row(g_ffn[i]), w_up[i].astype(BF16), conv_w[i].astype(F32),
                row(conv_b[i]), w_down[i].astype(BF16), row(g_ple[i]), w_ple_gate[i].astype(BF16),
                w_ple_proj[i].astype(BF16), row(g_final))
    return h
```
